```python
import jax
import jax.numpy as jnp
from jax import lax
import numpy as np

D_MODEL = 2048
BATCH = 2
SEQ = 4096
DEPTH = 2
DEC_BATCH = 32
DEC_SEQ = 1
PAST_LEN = 8192
PAGE_SIZE = 128

D_MIX = D_MODEL
HEAD_DIM = 128
H_A = D_MIX // 4 // HEAD_DIM
H_B = 4
POOL_GROUP_DIM = D_MIX // 4 // H_B
H_C = D_MIX // 2 // HEAD_DIM
W_A = H_A * HEAD_DIM
W_B = H_B * POOL_GROUP_DIM
W_C = H_C * HEAD_DIM
N_IN = 2 * W_A + W_B + 3 * W_C
SPLITS = (W_A, 2 * W_A, 2 * W_A + W_B, 2 * W_A + W_B + W_C, 2 * W_A + W_B + 2 * W_C)
CHUNK = 128
POOL_WINDOWS = (2, 4, 8, 16)
POOL_BUF = max(POOL_WINDOWS) - 1
Q_BLOCK = 128
SB_SCALE = HEAD_DIM ** -0.5
SB_BIAS_INIT = -8.0
D_FF = 5632
N_EXPERTS = 8
TOP_K = 2
MOE_BLOCK = 128
EPS = 1e-6

kernel_name = 'hybrid_gmlp_pool_stickbreak_decoder_step'


def rmsnorm(x, g):
    x32 = x.astype(jnp.float32)
    y = x32 * lax.rsqrt(jnp.mean(x32 * x32, axis=-1, keepdims=True) + EPS)
    return (y * g.astype(jnp.float32)).astype(x.dtype)


def gmlp_spatial_gate(u, v, w_s, b_s):
    L = u.shape[2]
    w = jnp.tril(w_s[:, :L, :L])
    mixed = jnp.einsum('hts,bcshd->bcthd', w, v) + jnp.transpose(b_s[:, :L])[None, None, :, :, None]
    return u * mixed


def multiscale_pool(p, buf, start_pos, pool_w, pool_scale):
    B, T, _ = p.shape
    full = jnp.concatenate([buf, p], axis=1)
    cs = jnp.cumsum(full.astype(jnp.float32), axis=1)
    cs = jnp.concatenate([jnp.zeros_like(cs[:, :1]), cs], axis=1)
    end = cs[:, POOL_BUF + 1:]
    pos = start_pos + jnp.arange(T, dtype=jnp.int32)
    means = []
    for g, w in enumerate(POOL_WINDOWS):
        sl = slice(g * POOL_GROUP_DIM, (g + 1) * POOL_GROUP_DIM)
        s = end[..., sl] - cs[:, POOL_BUF + 1 - w:POOL_BUF + 1 - w + T, sl]
        cnt = jnp.minimum(pos + 1, w).astype(jnp.float32)
        means.append(s / cnt[None, :, None])
    d = (jnp.concatenate(means, axis=-1) - p.astype(jnp.float32)).astype(p.dtype)
    d = d.reshape(B, T, H_B, POOL_GROUP_DIM)
    out = jnp.einsum('btgc,gcd->btgd', d, pool_w).reshape(B, T, W_B) * pool_scale
    return out, full[:, -POOL_BUF:]


def stickbreak_block(q, k, v, q_pos, k_pos, sb_bias):
    z = jnp.einsum('bqhd,bkhd->bhqk', q, k, preferred_element_type=jnp.float32) * SB_SCALE
    z = z + sb_bias.astype(jnp.float32)[None, :, None, None]
    mask = (k_pos[None, :] < q_pos[:, None])[None, None]
    log_1mb = jnp.where(mask, jax.nn.log_sigmoid(-z), 0.0)
    after = lax.cumsum(log_1mb, axis=3, reverse=True) - log_1mb
    a = jnp.where(mask, jnp.exp(jax.nn.log_sigmoid(z) + after), 0.0)
    return jnp.einsum('bhqk,bkhd->bqhd', a.astype(v.dtype), v)


def stickbreak_attention(q, k, v, q_pos, k_pos, sb_bias):
    B, Tq, H, D = q.shape
    qb = min(Q_BLOCK, Tq)
    nb = -(-Tq // qb)
    pad = nb * qb - Tq
    qp = jnp.pad(q, ((0, 0), (0, pad), (0, 0), (0, 0))).reshape(B, nb, qb, H, D).transpose(1, 0, 2, 3, 4)
    pp = jnp.pad(q_pos, (0, pad)).reshape(nb, qb)
    out = lax.map(lambda a: stickbreak_block(a[0], k, v, a[1], k_pos, sb_bias), (qp, pp))
    return out.transpose(1, 0, 2, 3, 4).reshape(B, nb * qb, H, D)[:, :Tq]


def token_mix(xn, start_pos, k_past, v_past, pool_buf, w_in, v_gain, w_s, b_s,
              pool_w, pool_scale, sb_bias, out_gain, w_out):
    B, T, _ = xn.shape
    u_a, v_a, p_b, q_c, k_c, v_c = jnp.split(xn @ w_in, SPLITS, axis=-1)
    L = min(T, CHUNK)
    u = jax.nn.gelu(u_a).reshape(B, T // L, L, H_A, HEAD_DIM)
    va = rmsnorm(jax.nn.gelu(v_a).reshape(B, T, H_A, HEAD_DIM), v_gain)
    o_a = gmlp_spatial_gate(u, va.reshape(B, T // L, L, H_A, HEAD_DIM), w_s, b_s).reshape(B, T, W_A)
    o_b, new_buf = multiscale_pool(p_b, pool_buf, start_pos, pool_w, pool_scale)
    q = q_c.reshape(B, T, H_C, HEAD_DIM)
    k = k_c.reshape(B, T, H_C, HEAD_DIM)
    v = v_c.reshape(B, T, H_C, HEAD_DIM)
    if k_past is None:
        k_all, v_all = k, v
    else:
        k_all = jnp.concatenate([k_past, k], axis=1)
        v_all = jnp.concatenate([v_past, v], axis=1)
    q_pos = start_pos + jnp.arange(T, dtype=jnp.int32)
    k_pos = jnp.arange(k_all.shape[1], dtype=jnp.int32)
    o_c = stickbreak_attention(q, k_all, v_all, q_pos, k_pos, sb_bias).reshape(B, T, W_C)
    merged = jnp.concatenate([rmsnorm(o_a, out_gain[:W_A]),
                              rmsnorm(o_b, out_gain[W_A:W_A + W_B]),
                              rmsnorm(o_c, out_gain[W_A + W_B:])], axis=-1)
    return merged @ w_out, k, v, new_buf, va


def swiglu(x, w_gate, w_up, w_down):
    return (jax.nn.silu(x @ w_gate) * (x @ w_up)) @ w_down


def moe_swiglu(x, router, w_gate, w_up, w_down):
    T, D = x.shape
    n_assign = T * TOP_K
    logits = jnp.dot(x, router, preferred_element_type=jnp.float32)
    top_logit, top_e = lax.top_k(logits, TOP_K)
    gate = jax.nn.softmax(top_logit, axis=-1)
    flat_e = top_e.reshape(-1)
    flat_tok = jnp.repeat(jnp.arange(T, dtype=jnp.int32), TOP_K)
    flat_g = gate.reshape(-1)
    order = jnp.argsort(flat_e, stable=True)
    se = flat_e[order]
    counts = jnp.bincount(flat_e, length=N_EXPERTS)
    starts = jnp.cumsum(counts) - counts
    pcounts = (counts + MOE_BLOCK - 1) // MOE_BLOCK * MOE_BLOCK
    pends = jnp.cumsum(pcounts)
    pstarts = pends - pcounts
    dest = pstarts[se] + jnp.arange(n_assign, dtype=jnp.int32) - starts[se]
    n_blocks = (n_assign + N_EXPERTS * (MOE_BLOCK - 1) + MOE_BLOCK - 1) // MOE_BLOCK
    n_slots = n_blocks * MOE_BLOCK
    slot_tok = jnp.full((n_slots,), T, jnp.int32).at[dest].set(flat_tok[order])
    slot_g = jnp.zeros((n_slots,), jnp.float32).at[dest].set(flat_g[order])
    block_e = jnp.minimum(jnp.searchsorted(pends, jnp.arange(n_blocks, dtype=jnp.int32) * MOE_BLOCK,
                                           side='right'), N_EXPERTS - 1)
    x_pad = jnp.concatenate([x, jnp.zeros((1, D), x.dtype)], axis=0)
    xb = x_pad[slot_tok].reshape(n_blocks, MOE_BLOCK, D)

    def expert_block(args):
        xe, e = args
        return swiglu(xe, w_gate[e], w_up[e], w_down[e])

    yb = lax.map(expert_block, (xb, block_e)).reshape(n_slots, D)
    y = jnp.zeros((T + 1, D), x.dtype).at[slot_tok].add((yb * slot_g[:, None]).astype(x.dtype))
    return y[:T]


def channel_mix(x, layer, ffn_w_gate, ffn_w_up, ffn_w_down, moe_router, moe_w_gate, moe_w_up, moe_w_down):
    i = layer // 2
    if layer % 2 == 0:
        return swiglu(x, ffn_w_gate[i], ffn_w_up[i], ffn_w_down[i])
    B, T, D = x.shape
    y = moe_swiglu(x.reshape(B * T, D), moe_router[i], moe_w_gate[i], moe_w_up[i], moe_w_down[i])
    return y.reshape(B, T, D)


def setup_inputs(seed: int = 0) -> dict:
    key = jax.random.key(seed)
    ks = jax.random.split(key, 25)
    f32 = jnp.float32
    n_pages = PAST_LEN // PAGE_SIZE
    n_phys = (5 * DEC_BATCH * n_pages) // 4
    n_dense = (DEPTH + 1) // 2
    n_moe = DEPTH // 2

    def nrm(k, shape, scale):
        return jax.random.normal(k, shape, f32) * scale

    def gain(k, shape, s=0.02):
        return 1.0 + s * jax.random.normal(k, shape, f32)

    page_table = jax.random.permutation(ks[0], n_phys)[:DEC_BATCH * n_pages]
    page_table = page_table.reshape(DEC_BATCH, n_pages).astype(jnp.int32)
    return {
        'x_prompt': nrm(ks[1], (BATCH, SEQ, D_MODEL), 1.0),
        'x_sample': nrm(ks[2], (DEC_BATCH, DEC_SEQ, D_MODEL), 1.0),
        'cache_k': nrm(ks[3], (DEPTH, n_phys, PAGE_SIZE, H_C, HEAD_DIM), 1.0),
        'cache_v': nrm(ks[4], (DEPTH, n_phys, PAGE_SIZE, H_C, HEAD_DIM), 1.0),
        'state_pool': nrm(ks[5], (DEPTH, DEC_BATCH, POOL_BUF, W_B), 1.0),
        'page_table': page_table,
        'norm_mix': gain(ks[6], (DEPTH, D_MODEL)),
        'w_in': nrm(ks[7], (DEPTH, D_MODEL, N_IN), D_MODEL ** -0.5),
        'gmlp_v_gain': gain(ks[8], (DEPTH, H_A, HEAD_DIM)),
        'gmlp_w': nrm(ks[9], (DEPTH, H_A, CHUNK, CHUNK), CHUNK ** -0.5),
        'gmlp_b': gain(ks[10], (DEPTH, H_A, CHUNK)),
        'pool_w': nrm(ks[11], (DEPTH, H_B, POOL_GROUP_DIM, POOL_GROUP_DIM), POOL_GROUP_DIM ** -0.5),
        'pool_scale': gain(ks[12], (DEPTH, W_B), 0.1),
        'sb_bias': SB_BIAS_INIT + 0.5 * jax.random.normal(ks[24], (DEPTH, H_C), f32),
        'out_gain': gain(ks[13], (DEPTH, D_MIX)),
        'w_out': nrm(ks[14], (DEPTH, D_MIX, D_MODEL), D_MIX ** -0.5),
        'norm_ffn': gain(ks[15], (DEPTH, D_MODEL)),
        'ffn_w_gate': nrm(ks[16], (n_dense, D_MODEL, D_FF), D_MODEL ** -0.5),
        'ffn_w_up': nrm(ks[17], (n_dense, D_MODEL, D_FF), D_MODEL ** -0.5),
        'ffn_w_down': nrm(ks[18], (n_dense, D_FF, D_MODEL), D_FF ** -0.5),
        'moe_router': nrm(ks[19], (n_moe, D_MODEL, N_EXPERTS), D_MODEL ** -0.5),
        'moe_w_gate': nrm(ks[20], (n_moe, N_EXPERTS, D_MODEL, D_FF), D_MODEL ** -0.5),
        'moe_w_up': nrm(ks[21], (n_moe, N_EXPERTS, D_MODEL, D_FF), D_MODEL ** -0.5),
        'moe_w_down': nrm(ks[22], (n_moe, N_EXPERTS, D_FF, D_MODEL), D_FF ** -0.5),
        'norm_final': gain(ks[23], (D_MODEL,)),
    }


def reference(x_prompt, x_sample, cache_k, cache_v, state_pool, page_table,
              norm_mix, w_in, gmlp_v_gain, gmlp_w, gmlp_b, pool_w, pool_scale, sb_bias, out_gain, w_out,
              norm_ffn, ffn_w_gate, ffn_w_up, ffn_w_down,
              moe_router, moe_w_gate, moe_w_up, moe_w_down, norm_final):
    hp, hs = x_prompt, x_sample
    n_dec = x_sample.shape[0]
    past_len = page_table.shape[1] * PAGE_SIZE
    zero_buf = jnp.zeros((x_prompt.shape[0], POOL_BUF, W_B), x_prompt.dtype)
    kp_l, vp_l, pp_l, ks_l, vs_l, ps_l, gs_l = [], [], [], [], [], [], []
    for l in range(DEPTH):
        mix_w = (w_in[l], gmlp_v_gain[l], gmlp_w[l], gmlp_b[l], pool_w[l], pool_scale[l], sb_bias[l],
                 out_gain[l], w_out[l])
        ffn_w = (ffn_w_gate, ffn_w_up, ffn_w_down, moe_router, moe_w_gate, moe_w_up, moe_w_down)
        o, k_new, v_new, buf_new, _ = token_mix(rmsnorm(hp, norm_mix[l]), 0, None, None, zero_buf, *mix_w)
        hp = hp + o
        hp = hp + channel_mix(rmsnorm(hp, norm_ffn[l]), l, *ffn_w)
        kp_l.append(k_new)
        vp_l.append(v_new)
        pp_l.append(buf_new)
        k_past = cache_k[l][page_table].reshape(n_dec, past_len, H_C, HEAD_DIM)
        v_past = cache_v[l][page_table].reshape(n_dec, past_len, H_C, HEAD_DIM)
        o, k_new, v_new, buf_new, va = token_mix(rmsnorm(hs, norm_mix[l]), past_len, k_past, v_past,
                                                 state_pool[l], *mix_w)
        hs = hs + o
        hs = hs + channel_mix(rmsnorm(hs, norm_ffn[l]), l, *ffn_w)
        ks_l.append(k_new)
        vs_l.append(v_new)
        ps_l.append(buf_new)
        gs_l.append(va)
    y_prompt = rmsnorm(hp, norm_final)
    y_sample = rmsnorm(hs, norm_final)
    return (y_prompt, y_sample,
            jnp.stack(kp_l), jnp.stack(vp_l), jnp.stack(pp_l),
            jnp.stack(ks_l), jnp.stack(vs_l), jnp.stack(ps_l), jnp.stack(gs_l))
```

```python
import functools

import jax
import jax.numpy as jnp
from jax import lax
from jax.experimental import pallas as pl
from jax.experimental.pallas import tpu as pltpu

F32 = jnp.float32
BF16 = jnp.bfloat16

HEAD_DIM = 128
CHUNK = 128
POOL_WINDOWS = (2, 4, 8, 16)
POOL_BUF = max(POOL_WINDOWS) - 1
POOL_HALO = 16
TOP_K = 2
EPS = 1e-6
SB_SCALE = HEAD_DIM ** -0.5
LOG2E = 1.4426950408889634
LANES = 128
TAIL = 256
ATTN_HEADS_PER_STEP = 2
CUMSUM_CHUNK = 256
PAGES_PER_STEP = 8
VMEM_LIMIT = 56 * 1024 * 1024


def _cparams(sem):
    return pltpu.CompilerParams(dimension_semantics=sem, vmem_limit_bytes=VMEM_LIMIT)


def _rms(x, g):
    ms = jnp.mean(x * x, axis=-1, keepdims=True)
    return x * lax.rsqrt(ms + EPS) * g


def _softplus(z):
    return jnp.maximum(z, 0.0) + jnp.log1p(jnp.exp(-jnp.abs(z)))


def _split2(x):
    hi = x.astype(BF16)
    return hi, (x - hi.astype(F32)).astype(BF16)


def _split3(x):
    h1 = x.astype(BF16)
    r1 = x - h1.astype(F32)
    h2 = r1.astype(BF16)
    h3 = (r1 - h2.astype(F32)).astype(BF16)
    return h1, h2, h3


def _dot(a, b):
    return jnp.dot(a, b, preferred_element_type=F32)


def _dot_hp(x, w):
    xh, xl = _split2(x)
    wh, wl = _split2(w)
    return _dot(xh, wh) + (_dot(xl, wh) + _dot(xh, wl))


def _dot3(parts, m):
    return sum(_dot(p, m) for p in parts)


def _row_tile(n_rows, cap):
    best = TAIL
    for k in range(1, n_rows // TAIL + 1):
        if n_rows % (k * TAIL) == 0 and k * TAIL <= cap:
            best = k * TAIL
    return best


def _norm_proj_body(x_ref, g_ref, w_ref, o_ref, xn_ref):
    @pl.when(pl.program_id(1) == 0)
    def _():
        xn_ref[...] = _rms(x_ref[...], g_ref[...]).astype(BF16)

    o_ref[...] = _dot(xn_ref[...], w_ref[...])


def _norm_proj(h, g, w, tm, tn):
    t, d = h.shape
    n = w.shape[1]
    return pl.pallas_call(
        _norm_proj_body,
        out_shape=jax.ShapeDtypeStruct((t, n), F32),
        grid=(t // tm, n // tn),
        in_specs=[pl.BlockSpec((tm, d), lambda i, j: (i, 0)),
                  pl.BlockSpec((1, d), lambda i, j: (0, 0)),
                  pl.BlockSpec((d, tn), lambda i, j: (0, j))],
        out_specs=pl.BlockSpec((tm, tn), lambda i, j: (i, j)),
        scratch_shapes=[pltpu.VMEM((tm, d), BF16)],
        compiler_params=_cparams(("parallel", "arbitrary")),
        name="norm_proj",
    )(h, g.reshape(1, d), w)


def _norm_proj_s_body(x_ref, g_ref, w_ref, o_ref):
    o_ref[...] = _dot_hp(_rms(x_ref[...], g_ref[...]), w_ref[...])


def _norm_proj_s(h, g, w, tn):
    t, d = h.shape
    n = w.shape[1]
    return pl.pallas_call(
        _norm_proj_s_body,
        out_shape=jax.ShapeDtypeStruct((t, n), F32),
        grid=(n // tn,),
        in_specs=[pl.BlockSpec((t, d), lambda j: (0, 0)),
                  pl.BlockSpec((1, d), lambda j: (0, 0)),
                  pl.BlockSpec((d, tn), lambda j: (0, j))],
        out_specs=pl.BlockSpec((t, tn), lambda j: (0, j)),
        compiler_params=_cparams(("parallel",)),
        name="norm_proj_sample",
    )(h, g.reshape(1, d), w)


def _mix_ab_body(ua_ref, va_ref, p_ref, pp_ref, vg_ref, ws_ref, bt_ref, pw_ref, ps_ref,
                 oa_ref, ob_ref, buf_ref, *, rows, n_heads):
    i = pl.program_id(1)
    r_i = lax.broadcasted_iota(jnp.int32, (CHUNK, CHUNK), 0)
    c_i = lax.broadcasted_iota(jnp.int32, (CHUNK, CHUNK), 1)
    causal = r_i >= c_i
    for c in range(rows // CHUNK):
        rs = slice(c * CHUNK, (c + 1) * CHUNK)
        for hd in range(n_heads):
            cs = slice(hd * HEAD_DIM, (hd + 1) * HEAD_DIM)
            u = jax.nn.gelu(ua_ref[rs, cs])
            van = _rms(jax.nn.gelu(va_ref[rs, cs]), vg_ref[:, cs])
            w = jnp.where(causal, ws_ref[hd], 0.0).astype(BF16)
            mixed = _dot(w, van.astype(BF16)) + bt_ref[:, hd:hd + 1]
            oa_ref[rs, cs] = u * mixed

    buf_ref[0:POOL_HALO, :] = jnp.where(i == 0, 0.0, pp_ref[...])
    buf_ref[POOL_HALO:POOL_HALO + rows, :] = p_ref[...]
    pos = i * rows + lax.broadcasted_iota(jnp.int32, (rows, 1), 0)
    for g, w in enumerate(POOL_WINDOWS):
        cs = slice(g * LANES, (g + 1) * LANES)
        s = buf_ref[POOL_HALO:POOL_HALO + rows, cs]
        for j in range(1, w):
            s = s + buf_ref[POOL_HALO - j:POOL_HALO - j + rows, cs]
        cnt = jnp.minimum(pos + 1, w).astype(F32)
        d = s / cnt - p_ref[:, cs]
        ob_ref[:, cs] = _dot(d.astype(BF16), pw_ref[g].astype(BF16)) * ps_ref[:, cs]


def _mix_ab_prompt(proj, n_batch, seq, v_gain, w_s, b_t, pool_w, pool_scale, w_a, w_b):
    rows = 256 if seq % 256 == 0 else CHUNK
    nr = seq // rows
    n_heads = w_a // HEAD_DIM
    halo_per_block = rows // POOL_HALO
    ca, cv, cp = 0, 1, 2 * w_a // w_b
    body = functools.partial(_mix_ab_body, rows=rows, n_heads=n_heads)
    return pl.pallas_call(
        body,
        out_shape=(jax.ShapeDtypeStruct((n_batch * seq, w_a), F32),
                   jax.ShapeDtypeStruct((n_batch * seq, w_b), F32)),
        grid=(n_batch, nr),
        in_specs=[pl.BlockSpec((rows, w_a), lambda b, i: (b * nr + i, ca)),
                  pl.BlockSpec((rows, w_a), lambda b, i: (b * nr + i, cv)),
                  pl.BlockSpec((rows, w_b), lambda b, i: (b * nr + i, cp)),
                  pl.BlockSpec((POOL_HALO, w_b),
                               lambda b, i: (jnp.maximum((b * nr + i) * halo_per_block - 1, 0), cp)),
                  pl.BlockSpec((1, w_a), lambda b, i: (0, 0)),
                  pl.BlockSpec(w_s.shape, lambda b, i: (0, 0, 0)),
                  pl.BlockSpec(b_t.shape, lambda b, i: (0, 0)),
                  pl.BlockSpec(pool_w.shape, lambda b, i: (0, 0, 0)),
                  pl.BlockSpec((1, w_b), lambda b, i: (0, 0))],
        out_specs=(pl.BlockSpec((rows, w_a), lambda b, i: (b * nr + i, 0)),
                   pl.BlockSpec((rows, w_b), lambda b, i: (b * nr + i, 0))),
        scratch_shapes=[pltpu.VMEM((POOL_HALO + rows, w_b), F32)],
        compiler_params=_cparams(("parallel", "arbitrary")),
        name="mix_ab_prompt",
    )(proj, proj, proj, proj, v_gain.reshape(1, w_a), w_s, b_t, pool_w, pool_scale.reshape(1, w_b))


def _mix_ab_sample_body(ua_ref, va_ref, p_ref, st_ref, vg_ref, w0_ref, b0_ref, pw_ref, ps_ref,
                        oa_ref, ob_ref, vn_ref, *, n_heads, past_len):
    for hd in range(n_heads):
        cs = slice(hd * HEAD_DIM, (hd + 1) * HEAD_DIM)
        u = jax.nn.gelu(ua_ref[:, cs])
        van = _rms(jax.nn.gelu(va_ref[:, cs]), vg_ref[:, cs])
        vn_ref[:, cs] = van
        oa_ref[:, cs] = u * (w0_ref[:, cs] * van + b0_ref[:, cs])
    for g, w in enumerate(POOL_WINDOWS):
        cs = slice(g * LANES, (g + 1) * LANES)
        s = p_ref[:, cs]
        for j in range(1, w):
            s = s + st_ref[:, POOL_BUF - j, cs]
        cnt = float(min(past_len + 1, w))
        d = s / cnt - p_ref[:, cs]
        ob_ref[:, cs] = _dot_hp(d, pw_ref[g]) * ps_ref[:, cs]


def _mix_ab_sample(ua, va, p, state, v_gain, w0, b0, pool_w, pool_scale, past_len):
    n, w_a = ua.shape
    w_b = p.shape[1]
    body = functools.partial(_mix_ab_sample_body, n_heads=w_a // HEAD_DIM, past_len=past_len)
    return pl.pallas_call(
        body,
        out_shape=(jax.ShapeDtypeStruct((n, w_a), F32), jax.ShapeDtypeStruct((n, w_b), F32),
                   jax.ShapeDtypeStruct((n, w_a), F32)),
        name="mix_ab_sample",
    )(ua, va, p, state, v_gain.reshape(1, w_a), w0, b0, pool_w, pool_scale.reshape(1, w_b))


def _attn_body(bias_ref, q_ref, k_ref, v_ref, o_ref, acc_ref, *car_refs, tq, nh):
    hg = pl.program_id(1)
    qi = pl.program_id(2)
    ch = CUMSUM_CHUNK
    n_ch = tq // ch
    r_i = lax.broadcasted_iota(jnp.int32, (2 * ch, ch), 0)
    c_i = lax.broadcasted_iota(jnp.int32, (2 * ch, ch), 1)
    later2 = (jnp.where(r_i >= ch, r_i - ch, r_i) > c_i).astype(BF16)
    visible = lax.broadcasted_iota(jnp.int32, (tq, tq), 1) < lax.broadcasted_iota(jnp.int32, (tq, tq), 0)
    qs = [q_ref[:, hh * HEAD_DIM:(hh + 1) * HEAD_DIM].astype(BF16) for hh in range(nh)]
    bias2 = [bias_ref[hg * nh + hh] * LOG2E for hh in range(nh)]

    def block(kj, masked):
        off = pl.multiple_of(kj * tq, tq)
        for hh in range(nh):
            cs = slice(hh * HEAD_DIM, (hh + 1) * HEAD_DIM)
            kb = k_ref[pl.ds(off, tq), cs].astype(BF16)
            vb = v_ref[pl.ds(off, tq), cs].astype(BF16)
            z2 = lax.dot_general(qs[hh], kb, (((1,), (1,)), ((), ())),
                                 preferred_element_type=F32) * (SB_SCALE * LOG2E) + bias2[hh]
            sp2 = jnp.maximum(z2, 0.0) + jnp.log2(1.0 + jnp.exp2(-jnp.abs(z2)))
            log2_b = z2 - sp2
            if masked:
                sp2 = jnp.where(visible, sp2, 0.0)
            hi, lo = _split2(sp2)
            stacked = jnp.concatenate(
                [jnp.concatenate([hi[:, c * ch:(c + 1) * ch], lo[:, c * ch:(c + 1) * ch]], axis=1)
                 for c in range(n_ch)], axis=0)
            within = _dot(stacked, later2)
            run = car_refs[hh][...]
            after = [None] * n_ch
            for c in reversed(range(n_ch)):
                after[c] = within[c * tq:(c + 1) * tq, :] + run
                run = run + jnp.sum(sp2[:, c * ch:(c + 1) * ch], axis=1, keepdims=True)
            car_refs[hh][...] = run
            a = jnp.exp2(log2_b - jnp.concatenate(after, axis=1))
            if masked:
                a = jnp.where(visible, a, 0.0)
            acc_ref[:, cs] += _dot(a.astype(BF16), vb)

    acc_ref[...] = jnp.zeros_like(acc_ref)
    for car_ref in car_refs:
        car_ref[...] = jnp.zeros_like(car_ref)
    block(qi, True)

    def body(n, carry):
        block(qi - 1 - n, False)
        return carry

    lax.fori_loop(0, qi, body, 0)
    o_ref[...] = acc_ref[...]


def _attn_prompt(proj, sb_bias, n_batch, seq, n_heads, col_q, col_k, col_v):
    tq = next(t for t in (512, 256, 128) if seq % t == 0)
    nq = seq // tq
    nh = ATTN_HEADS_PER_STEP
    assert n_heads % nh == 0 and col_q % nh == 0 and col_k % nh == 0 and col_v % nh == 0
    wq = nh * HEAD_DIM
    body = functools.partial(_attn_body, tq=tq, nh=nh)
    return pl.pallas_call(
        body,
        out_shape=jax.ShapeDtypeStruct((n_batch * seq, n_heads * HEAD_DIM), F32),
        grid_spec=pltpu.PrefetchScalarGridSpec(
            num_scalar_prefetch=1,
            grid=(n_batch, n_heads // nh, nq),
            in_specs=[pl.BlockSpec((tq, wq), lambda b, h, i, s: (b * nq + i, col_q // nh + h)),
                      pl.BlockSpec((seq, wq), lambda b, h, i, s: (b, col_k // nh + h)),
                      pl.BlockSpec((seq, wq), lambda b, h, i, s: (b, col_v // nh + h))],
            out_specs=pl.BlockSpec((tq, wq), lambda b, h, i, s: (b * nq + i, h)),
            scratch_shapes=[pltpu.VMEM((tq, wq), F32)] + [pltpu.VMEM((tq, 1), F32)] * nh),
        compiler_params=_cparams(("parallel", "parallel", "arbitrary")),
        name="attn_prompt",
    )(sb_bias, proj, proj, proj)


def _paged_body(pt_ref, q_ref, bias_ref, later_ref, same_ref, *rest, n_groups, n_heads):
    del pt_ref
    pps = PAGES_PER_STEP
    k_refs = rest[:pps]
    v_refs = rest[pps:2 * pps]
    o_ref, z_ref, a_ref, acc_ref = rest[2 * pps:]
    s = pl.program_id(1)
    page_rows = k_refs[0].shape[1] * n_heads
    lane = lax.broadcasted_iota(jnp.int32, (2 * n_heads, page_rows), 1)
    sub = lax.broadcasted_iota(jnp.int32, (2 * n_heads, page_rows), 0)
    own_head = (lane % n_heads) == (sub % n_heads)
    nt = (((1,), (1,)), ((), ()))

    @pl.when(s < n_groups)
    def _():
        qb = jnp.concatenate(_split2(q_ref[0]), axis=0)
        for i in range(pps):
            kh, kl = _split2(k_refs[i][0].reshape(page_rows, HEAD_DIM))
            s16 = (lax.dot_general(qb, kh, nt, preferred_element_type=F32)
                   + lax.dot_general(qb, kl, nt, preferred_element_type=F32))
            z_ref[pl.ds(s * pps + i, 1), :] = jnp.sum(jnp.where(own_head, s16, 0.0), axis=0, keepdims=True)

    @pl.when(s == n_groups - 1)
    def _():
        n_pages = z_ref.shape[0]
        z = z_ref[...] * SB_SCALE + bias_ref[...]
        sp = _softplus(z)
        log_1mb = -sp
        log_b = z - sp
        parts = _split3(log_1mb)
        within = _dot3(parts, later_ref[...])
        totals = _dot3(parts, same_ref[...])
        p_r = lax.broadcasted_iota(jnp.int32, (n_pages, n_pages), 0)
        p_c = lax.broadcasted_iota(jnp.int32, (n_pages, n_pages), 1)
        later_page = (p_c > p_r).astype(BF16)
        carry = sum(_dot(later_page, t) for t in _split3(totals))
        a_ref[...] = jnp.exp(log_b + within + carry)
        acc_ref[...] = jnp.zeros_like(acc_ref)

    @pl.when(s >= n_groups)
    def _():
        acc = acc_ref[...]
        for i in range(pps):
            ah, al = _split2(a_ref[pl.ds((s - n_groups) * pps + i, 1), :])
            rows = jnp.concatenate([jnp.broadcast_to(ah, (n_heads, page_rows)),
                                    jnp.broadcast_to(al, (n_heads, page_rows))], axis=0)
            am = jnp.where(own_head, rows, jnp.zeros_like(rows))
            vh, vl = _split2(v_refs[i][0].reshape(page_rows, HEAD_DIM))
            acc = acc + (_dot(am, vh) + _dot(am, vl))
        acc_ref[...] = acc

    @pl.when(s == 2 * n_groups - 1)
    def _():
        o_ref[0] = acc_ref[0:n_heads, :] + acc_ref[n_heads:, :]


def _attn_sample(q, cache_k, cache_v, layer, page_table, sb_bias):
    n_dec, n_heads, _ = q.shape
    n_pages = page_table.shape[1]
    page = cache_k.shape[2]
    pps = PAGES_PER_STEP
    assert n_pages % pps == 0
    n_groups = n_pages // pps
    page_rows = page * n_heads
    idx = jnp.arange(page_rows, dtype=jnp.int32)
    same_head = (idx[:, None] % n_heads) == (idx[None, :] % n_heads)
    later = (same_head & (idx[:, None] // n_heads > idx[None, :] // n_heads)).astype(BF16)
    same = same_head.astype(BF16)
    bias_lane = jnp.tile(sb_bias.astype(F32), page).reshape(1, page_rows)

    def k_map(i):
        return lambda b, s, pt: (layer, pt[b, jnp.minimum(s, n_groups - 1) * pps + i], 0, 0, 0)

    def v_map(i):
        return lambda b, s, pt: (layer, pt[b, jnp.maximum(s - n_groups, 0) * pps + i], 0, 0, 0)

    page_block = (None, 1, page, n_heads, HEAD_DIM)
    body = functools.partial(_paged_body, n_groups=n_groups, n_heads=n_heads)
    return pl.pallas_call(
        body,
        out_shape=jax.ShapeDtypeStruct((n_dec, n_heads, HEAD_DIM), F32),
        grid_spec=pltpu.PrefetchScalarGridSpec(
            num_scalar_prefetch=1,
            grid=(n_dec, 2 * n_groups),
            in_specs=([pl.BlockSpec((1, n_heads, HEAD_DIM), lambda b, s, pt: (b, 0, 0)),
                       pl.BlockSpec((1, page_rows), lambda b, s, pt: (0, 0)),
                       pl.BlockSpec((page_rows, page_rows), lambda b, s, pt: (0, 0)),
                       pl.BlockSpec((page_rows, page_rows), lambda b, s, pt: (0, 0))]
                      + [pl.BlockSpec(page_block, k_map(i)) for i in range(pps)]
                      + [pl.BlockSpec(page_block, v_map(i)) for i in range(pps)]),
            out_specs=pl.BlockSpec((1, n_heads, HEAD_DIM), lambda b, s, pt: (b, 0, 0)),
            scratch_shapes=[pltpu.VMEM((n_pages, page_rows), F32),
                            pltpu.VMEM((n_pages, page_rows), F32),
                            pltpu.VMEM((2 * n_heads, HEAD_DIM), F32)]),
        compiler_params=_cparams(("arbitrary", "arbitrary")),
        name="attn_sample",
    )(page_table, q, bias_lane, later, same, *([cache_k] * pps), *([cache_v] * pps))


def _merged(oa_ref, ob_ref, oc_ref, g_ref, w_a, w_b):
    return (_rms(oa_ref[...], g_ref[:, 0:w_a]), _rms(ob_ref[...], g_ref[:, w_a:w_a + w_b]),
            _rms(oc_ref[...], g_ref[:, w_a + w_b:]))


def _out_proj_body(oa_ref, ob_ref, oc_ref, h_ref, g_ref, w_ref, gn_ref, ho_ref, xn_ref, *, w_a, w_b):
    a, b, c = _merged(oa_ref, ob_ref, oc_ref, g_ref, w_a, w_b)
    acc = _dot(a.astype(BF16), w_ref[0:w_a, :])
    acc = acc + _dot(b.astype(BF16), w_ref[w_a:w_a + w_b, :])
    acc = acc + _dot(c.astype(BF16), w_ref[w_a + w_b:, :])
    hn = h_ref[...] + acc
    ho_ref[...] = hn
    xn_ref[...] = _rms(hn, gn_ref[...]).astype(xn_ref.dtype)


def _out_proj(o_a, o_b, o_c, h, out_gain, w_out, norm_next, xn_dtype, tm):
    t, d = h.shape
    w_a, w_b, w_c = o_a.shape[1], o_b.shape[1], o_c.shape[1]
    d_mix = w_a + w_b + w_c
    body = functools.partial(_out_proj_body, w_a=w_a, w_b=w_b)
    row = lambda i: (i, 0)
    fixed = lambda i: (0, 0)
    return pl.pallas_call(
        body,
        out_shape=(jax.ShapeDtypeStruct((t, d), F32), jax.ShapeDtypeStruct((t, d), xn_dtype)),
        grid=(t // tm,),
        in_specs=[pl.BlockSpec((tm, w_a), row), pl.BlockSpec((tm, w_b), row), pl.BlockSpec((tm, w_c), row),
                  pl.BlockSpec((tm, d), row), pl.BlockSpec((1, d_mix), fixed),
                  pl.BlockSpec((d_mix, d), fixed), pl.BlockSpec((1, d), fixed)],
        out_specs=(pl.BlockSpec((tm, d), row), pl.BlockSpec((tm, d), row)),
        compiler_params=_cparams(("parallel",)),
        name="out_proj",
    )(o_a, o_b, o_c, h, out_gain.reshape(1, d_mix), w_out, norm_next.reshape(1, d))


def _out_proj_s_body(oa_ref, ob_ref, oc_ref, h_ref, g_ref, w_ref, ho_ref, *, w_a, w_b):
    merged = jnp.concatenate(_merged(oa_ref, ob_ref, oc_ref, g_ref, w_a, w_b), axis=1)
    ho_ref[...] = h_ref[...] + _dot_hp(merged, w_ref[...])


def _out_proj_s(o_a, o_b, o_c, h, out_gain, w_out, tn):
    t, d = h.shape
    w_a, w_b, w_c = o_a.shape[1], o_b.shape[1], o_c.shape[1]
    d_mix = w_a + w_b + w_c
    body = functools.partial(_out_proj_s_body, w_a=w_a, w_b=w_b)
    fixed = lambda j: (0, 0)
    col = lambda j: (0, j)
    return pl.pallas_call(
        body,
        out_shape=jax.ShapeDtypeStruct((t, d), F32),
        grid=(d // tn,),
        in_specs=[pl.BlockSpec((t, w_a), fixed), pl.BlockSpec((t, w_b), fixed), pl.BlockSpec((t, w_c), fixed),
                  pl.BlockSpec((t, tn), col), pl.BlockSpec((1, d_mix), fixed), pl.BlockSpec((d_mix, tn), col)],
        out_specs=pl.BlockSpec((t, tn), col),
        compiler_params=_cparams(("parallel",)),
        name="out_proj_sample",
    )(o_a, o_b, o_c, h, out_gain.reshape(1, d_mix), w_out)


def _ffn_body(x_ref, h_ref, wg_ref, wu_ref, wd_ref, o_ref):
    @pl.when(pl.program_id(1) == 0)
    def _():
        o_ref[...] = h_ref[...]

    x = x_ref[...]
    hid = (jax.nn.silu(_dot(x, wg_ref[...])) * _dot(x, wu_ref[...])).astype(BF16)
    o_ref[...] += _dot(hid, wd_ref[...])


def _ffn_dense(xn, h, w_gate, w_up, w_down, tm, tf):
    t, d = h.shape
    d_ff = w_gate.shape[1]
    return pl.pallas_call(
        _ffn_body,
        out_shape=jax.ShapeDtypeStruct((t, d), F32),
        grid=(t // tm, d_ff // tf),
        in_specs=[pl.BlockSpec((tm, d), lambda i, f: (i, 0)),
                  pl.BlockSpec((tm, d), lambda i, f: (i, 0)),
                  pl.BlockSpec((d, tf), lambda i, f: (0, f)),
                  pl.BlockSpec((d, tf), lambda i, f: (0, f)),
                  pl.BlockSpec((tf, d), lambda i, f: (f, 0))],
        out_specs=pl.BlockSpec((tm, d), lambda i, f: (i, 0)),
        compiler_params=_cparams(("parallel", "arbitrary")),
        name="ffn_dense",
    )(xn, h, w_gate, w_up, w_down)


def _ffn_s_body(h_ref, gn_ref, wg_ref, wu_ref, wd_ref, o_ref):
    @pl.when(pl.program_id(0) == 0)
    def _():
        o_ref[...] = h_ref[...]

    xn = _rms(h_ref[...], gn_ref[...])
    hid = jax.nn.silu(_dot_hp(xn, wg_ref[...])) * _dot_hp(xn, wu_ref[...])
    o_ref[...] += _dot_hp(hid, wd_ref[...])


def _ffn_s(h, norm, w_gate, w_up, w_down, tf):
    t, d = h.shape
    d_ff = w_gate.shape[1]
    fixed = lambda f: (0, 0)
    return pl.pallas_call(
        _ffn_s_body,
        out_shape=jax.ShapeDtypeStruct((t, d), F32),
        grid=(d_ff // tf,),
        in_specs=[pl.BlockSpec((t, d), fixed), pl.BlockSpec((1, d), fixed),
                  pl.BlockSpec((d, tf), lambda f: (0, f)),
                  pl.BlockSpec((d, tf), lambda f: (0, f)),
                  pl.BlockSpec((tf, d), lambda f: (f, 0))],
        out_specs=pl.BlockSpec((t, d), fixed),
        compiler_params=_cparams(("arbitrary",)),
        name="ffn_sample",
    )(h, norm.reshape(1, d), w_gate, w_up, w_down)


def _router_body(xp_ref, xs_ref, rt_ref, meta_ref, cnt_ref, carry_ref, *, tm, n_exp, n_tok):
    i = pl.program_id(0)

    @pl.when(i == 0)
    def _():
        carry_ref[...] = jnp.zeros_like(carry_ref)

    x = jnp.where(i == pl.num_programs(0) - 1, xs_ref[...], xp_ref[...])
    lane = lax.broadcasted_iota(jnp.int32, (tm, LANES), 1)
    neg = jnp.float32(-jnp.inf)
    logits = jnp.full((tm, LANES), neg, F32)
    for e in range(n_exp):
        s = jnp.sum(x * rt_ref[e:e + 1, :], axis=-1, keepdims=True)
        logits = jnp.where(lane == e, s, logits)
    m1 = jnp.max(logits, axis=-1, keepdims=True)
    i1 = jnp.min(jnp.where(logits == m1, lane, LANES), axis=-1, keepdims=True)
    rest = jnp.where(lane == i1, neg, logits)
    m2 = jnp.max(rest, axis=-1, keepdims=True)
    i2 = jnp.min(jnp.where(rest == m2, lane, LANES), axis=-1, keepdims=True)
    e2 = jnp.exp(m2 - m1)
    den = 1.0 + e2
    g1 = 1.0 / den
    g2 = e2 / den
    row = i * tm + lax.broadcasted_iota(jnp.int32, (tm, 1), 0)
    chosen = ((lane == i1) | (lane == i2)) & (row < n_tok)
    onehot = jnp.where(chosen, 1.0, 0.0)
    r_i = lax.broadcasted_iota(jnp.int32, (tm, tm), 0)
    c_i = lax.broadcasted_iota(jnp.int32, (tm, tm), 1)
    earlier = (c_i < r_i).astype(BF16)
    before = _dot(earlier, onehot.astype(BF16)) + carry_ref[...]
    r1 = jnp.sum(jnp.where(lane == i1, before, 0.0), axis=-1, keepdims=True)
    r2 = jnp.sum(jnp.where(lane == i2, before, 0.0), axis=-1, keepdims=True)
    carry_ref[...] += jnp.sum(onehot, axis=0, keepdims=True)
    meta = jnp.zeros((tm, LANES), F32)
    for k, val in enumerate((i1.astype(F32), i2.astype(F32), g1, g2, r1, r2)):
        meta = jnp.where(lane == k, val, meta)
    meta_ref[...] = meta
    cnt_ref[...] = carry_ref[...]


def _router(xn_p, xn_s, router_t, n_tok):
    tm = TAIL
    n_p, d = xn_p.shape
    n_exp = router_t.shape[0]
    n_blocks = n_p // tm + 1
    body = functools.partial(_router_body, tm=tm, n_exp=n_exp, n_tok=n_tok)
    return pl.pallas_call(
        body,
        out_shape=(jax.ShapeDtypeStruct((n_blocks * tm, LANES), F32), jax.ShapeDtypeStruct((1, LANES), F32)),
        grid=(n_blocks,),
        in_specs=[pl.BlockSpec((tm, d), lambda i: (jnp.minimum(i, n_blocks - 2), 0)),
                  pl.BlockSpec((tm, d), lambda i: (0, 0)),
                  pl.BlockSpec((n_exp, d), lambda i: (0, 0))],
        out_specs=(pl.BlockSpec((tm, LANES), lambda i: (i, 0)), pl.BlockSpec((1, LANES), lambda i: (0, 0))),
        scratch_shapes=[pltpu.VMEM((1, LANES), F32)],
        compiler_params=_cparams(("arbitrary",)),
        name="moe_router",
    )(xn_p, xn_s, router_t)


def _dispatch_body(tok_ref, nb_ref, xp_hbm, xs_hbm, o_ref, buf, sem, *, tm, n_p):
    b = pl.program_id(0)

    def issue(r, carry):
        tok = tok_ref[b * tm + r]
        dst = buf.at[pl.ds(r, 1), :]

        @pl.when(tok < n_p)
        def _():
            pltpu.make_async_copy(xp_hbm.at[pl.ds(tok, 1), :], dst, sem).start()

        @pl.when(tok >= n_p)
        def _():
            pltpu.make_async_copy(xs_hbm.at[pl.ds(tok - n_p, 1), :], dst, sem).start()

        return carry

    def drain(r, carry):
        pltpu.make_async_copy(xs_hbm.at[pl.ds(0, 1), :], buf.at[pl.ds(r, 1), :], sem).wait()
        return carry

    @pl.when(b < nb_ref[0])
    def _():
        lax.fori_loop(0, tm, issue, 0)
        lax.fori_loop(0, tm, drain, 0)
        o_ref[...] = buf[...].astype(BF16)

    @pl.when(b >= nb_ref[0])
    def _():
        o_ref[...] = jnp.zeros_like(o_ref)


def _dispatch(slot_tok, nb_used, xn_p, xn_s, tm):
    n_slots = slot_tok.shape[0]
    n_p, d = xn_p.shape
    body = functools.partial(_dispatch_body, tm=tm, n_p=n_p)
    return pl.pallas_call(
        body,
        out_shape=jax.ShapeDtypeStruct((n_slots, d), BF16),
        grid_spec=pltpu.PrefetchScalarGridSpec(
            num_scalar_prefetch=2,
            grid=(n_slots // tm,),
            in_specs=[pl.BlockSpec(memory_space=pl.ANY), pl.BlockSpec(memory_space=pl.ANY)],
            out_specs=pl.BlockSpec((tm, d), lambda b, st, nu: (b, 0)),
            scratch_shapes=[pltpu.VMEM((tm, d), F32), pltpu.SemaphoreType.DMA(())]),
        compiler_params=_cparams(("arbitrary",)),
        name="moe_dispatch",
    )(slot_tok, nb_used, xn_p, xn_s)


def _moe_body(be_ref, nb_ref, x_ref, wg_ref, wu_ref, wd_ref, o_ref):
    del be_ref
    b = pl.program_id(0)
    f = pl.program_id(1)

    @pl.when(f == 0)
    def _():
        o_ref[...] = jnp.zeros_like(o_ref)

    @pl.when(b < nb_ref[0])
    def _():
        xb = x_ref[...]
        hid = (jax.nn.silu(_dot(xb, wg_ref[...])) * _dot(xb, wu_ref[...])).astype(BF16)
        o_ref[...] += _dot(hid, wd_ref[...])


def _moe_experts(blk_expert, nb_used, slots, w_gate, w_up, w_down, tm, tf):
    n_slots, d = slots.shape
    d_ff = w_gate.shape[2]
    nf = d_ff // tf
    nb = n_slots // tm

    def w_in_map(b, f, be, nu):
        live = b < nu[0]
        return be[jnp.minimum(b, nu[0] - 1)], 0, jnp.where(live, f, nf - 1)

    def w_out_map(b, f, be, nu):
        live = b < nu[0]
        return be[jnp.minimum(b, nu[0] - 1)], jnp.where(live, f, nf - 1), 0

    return pl.pallas_call(
        _moe_body,
        out_shape=jax.ShapeDtypeStruct((n_slots, d), F32),
        grid_spec=pltpu.PrefetchScalarGridSpec(
            num_scalar_prefetch=2,
            grid=(nb, nf),
            in_specs=[pl.BlockSpec((tm, d), lambda b, f, be, nu: (b, 0)),
                      pl.BlockSpec((None, d, tf), w_in_map),
                      pl.BlockSpec((None, d, tf), w_in_map),
                      pl.BlockSpec((None, tf, d), w_out_map)],
            out_specs=pl.BlockSpec((tm, d), lambda b, f, be, nu: (b, 0))),
        compiler_params=_cparams(("arbitrary", "arbitrary")),
        name="moe_experts",
    )(blk_expert, nb_used, slots, w_gate, w_up, w_down)


def _combine_body(dest_ref, y_hbm, hp_ref, hs_ref, meta_ref, op_ref, os_ref, buf0, buf1, sem, *, tm, n_tok):
    i = pl.program_id(0)
    is_tail = i == pl.num_programs(0) - 1
    base = i * tm

    def row_copy(r, k, buf):
        t = jnp.minimum(base + r, n_tok - 1)
        return pltpu.make_async_copy(y_hbm.at[pl.ds(dest_ref[t * TOP_K + k], 1), :],
                                     buf.at[pl.ds(r, 1), :], sem)

    def issue(r, carry):
        row_copy(r, 0, buf0).start()
        row_copy(r, 1, buf1).start()
        return carry

    def drain(r, carry):
        row_copy(r, 0, buf0).wait()
        row_copy(r, 1, buf1).wait()
        return carry

    lax.fori_loop(0, tm, issue, 0)
    lax.fori_loop(0, tm, drain, 0)
    valid = (base + lax.broadcasted_iota(jnp.int32, (tm, 1), 0)) < n_tok
    g1 = jnp.where(valid, meta_ref[:, 2:3], 0.0)
    g2 = jnp.where(valid, meta_ref[:, 3:4], 0.0)
    y = buf0[...] * g1 + buf1[...] * g2

    @pl.when(is_tail)
    def _():
        os_ref[...] = hs_ref[...] + y

    @pl.when(jnp.logical_not(is_tail))
    def _():
        op_ref[...] = hp_ref[...] + y


def _combine(dest, y_slots, h_p, h_s, meta, n_tok):
    tm = TAIL
    n_p, d = h_p.shape
    n_blocks = n_p // tm + 1
    body = functools.partial(_combine_body, tm=tm, n_tok=n_tok)
    return pl.pallas_call(
        body,
        out_shape=(jax.ShapeDtypeStruct((n_p, d), F32), jax.ShapeDtypeStruct((tm, d), F32)),
        grid_spec=pltpu.PrefetchScalarGridSpec(
            num_scalar_prefetch=1,
            grid=(n_blocks,),
            in_specs=[pl.BlockSpec(memory_space=pl.ANY),
                      pl.BlockSpec((tm, d), lambda i, ds: (jnp.minimum(i, n_blocks - 2), 0)),
                      pl.BlockSpec((tm, d), lambda i, ds: (0, 0)),
                      pl.BlockSpec((tm, LANES), lambda i, ds: (i, 0))],
            out_specs=(pl.BlockSpec((tm, d), lambda i, ds: (jnp.minimum(i, n_blocks - 2), 0)),
                       pl.BlockSpec((tm, d), lambda i, ds: (0, 0))),
            scratch_shapes=[pltpu.VMEM((tm, d), F32), pltpu.VMEM((tm, d), F32),
                            pltpu.SemaphoreType.DMA(())]),
        compiler_params=_cparams(("arbitrary",)),
        name="moe_combine",
    )(dest, y_slots, h_p, h_s, meta)


def _moe_layer(xn_p, xn_s, h_p, h_s, router, w_gate, w_up, w_down, n_tok, tm_e, tf):
    n_exp = router.shape[1]
    meta, cnt = _router(xn_p, xn_s, jnp.transpose(router), n_tok)
    expert = meta[:n_tok, 0:TOP_K].astype(jnp.int32)
    rank = meta[:n_tok, 4:4 + TOP_K].astype(jnp.int32)
    counts = cnt[0, :n_exp].astype(jnp.int32)
    padded = (counts + tm_e - 1) // tm_e * tm_e
    ends = jnp.cumsum(padded)
    starts = ends - padded
    dest = (starts[expert] + rank).reshape(-1)
    n_assign = n_tok * TOP_K
    nb = -(-(n_assign + n_exp * (tm_e - 1)) // tm_e)
    blk_expert = jnp.minimum(
        jnp.searchsorted(ends, jnp.arange(nb, dtype=jnp.int32) * tm_e, side='right'), n_exp - 1).astype(jnp.int32)
    nb_used = (ends[-1:] // tm_e).astype(jnp.int32)
    tok = jnp.repeat(jnp.arange(n_tok, dtype=jnp.int32), TOP_K)
    slot_tok = jnp.full((nb * tm_e,), n_tok, jnp.int32).at[dest].set(tok)
    slots = _dispatch(slot_tok, nb_used, xn_p, xn_s, tm_e)
    y_slots = _moe_experts(blk_expert, nb_used, slots, w_gate, w_up, w_down, tm_e, tf)
    return _combine(dest, y_slots, h_p, h_s, meta, n_tok)


def _norm_body(x_ref, g_ref, o_ref):
    o_ref[...] = _rms(x_ref[...], g_ref[...])


def _norm(h, g, tm):
    t, d = h.shape
    return pl.pallas_call(
        _norm_body,
        out_shape=jax.ShapeDtypeStruct((t, d), F32),
        grid=(t // tm,),
        in_specs=[pl.BlockSpec((tm, d), lambda i: (i, 0)), pl.BlockSpec((1, d), lambda i: (0, 0))],
        out_specs=pl.BlockSpec((tm, d), lambda i: (i, 0)),
        compiler_params=_cparams(("parallel",)),
        name="row_norm",
    )(h, g.reshape(1, d))


def kernel(x_prompt, x_sample, cache_k, cache_v, state_pool, page_table, norm_mix, w_in, gmlp_v_gain, gmlp_w,
           gmlp_b, pool_w, pool_scale, sb_bias, out_gain, w_out, norm_ffn, ffn_w_gate, ffn_w_up, ffn_w_down,
           moe_router, moe_w_gate, moe_w_up, moe_w_down, norm_final):
    n_batch, seq, d_model = x_prompt.shape
    n_dec, dec_seq, _ = x_sample.shape
    depth = norm_mix.shape[0]
    page = cache_k.shape[2]
    h_c = cache_k.shape[3]
    past_len = page_table.shape[1] * page
    w_a = gmlp_v_gain.shape[1] * HEAD_DIM
    w_b = pool_scale.shape[1]
    w_c = h_c * HEAD_DIM
    d_ff = ffn_w_gate.shape[2]
    assert dec_seq == 1 and n_dec < TAIL and seq % CHUNK == 0 and (n_batch * seq) % TAIL == 0
    assert past_len % CHUNK == 0 and w_a == w_b

    n_p = n_batch * seq
    n_tok = n_p + n_dec
    col_q = (2 * w_a + w_b) // HEAD_DIM
    col_k = col_q + h_c
    col_v = col_k + h_c
    o_q, o_k, o_v = col_q * HEAD_DIM, col_k * HEAD_DIM, col_v * HEAD_DIM
    tm_big = _row_tile(n_p, 1024)
    tf = 512 if d_ff % 512 == 0 else 256
    tn = 512

    def tail_rows(x):
        return jnp.pad(x, ((0, TAIL - n_dec), (0, 0)))

    h_p = x_prompt.reshape(n_p, d_model)
    h_s = tail_rows(x_sample.reshape(n_dec, d_model))

    kp_l, vp_l, pp_l, ks_l, vs_l, ps_l, gs_l = [], [], [], [], [], [], []
    for l in range(depth):
        proj = _norm_proj(h_p, norm_mix[l], w_in[l].astype(BF16), tm_big, tn)
        o_a, o_b = _mix_ab_prompt(proj, n_batch, seq, gmlp_v_gain[l], gmlp_w[l],
                                  jnp.transpose(gmlp_b[l]), pool_w[l], pool_scale[l], w_a, w_b)
        o_c = _attn_prompt(proj, sb_bias[l], n_batch, seq, h_c, col_q, col_k, col_v)
        is_moe = l % 2 == 1
        h_p, xn_p = _out_proj(o_a, o_b, o_c, h_p, out_gain[l], w_out[l].astype(BF16), norm_ffn[l],
                              F32 if is_moe else BF16, TAIL)

        ps = _norm_proj_s(h_s, norm_mix[l], w_in[l], tn)[:n_dec]
        p_s = ps[:, 2 * w_a:2 * w_a + w_b]
        w0 = jnp.repeat(gmlp_w[l, :, 0, 0], HEAD_DIM).reshape(1, w_a)
        b0 = jnp.repeat(gmlp_b[l, :, 0], HEAD_DIM).reshape(1, w_a)
        oa_s, ob_s, va_s = _mix_ab_sample(ps[:, 0:w_a], ps[:, w_a:2 * w_a], p_s, state_pool[l],
                                          gmlp_v_gain[l], w0, b0, pool_w[l], pool_scale[l], past_len)
        q_s = ps[:, o_q:o_k].reshape(n_dec, h_c, HEAD_DIM)
        oc_s = _attn_sample(q_s, cache_k, cache_v, l, page_table, sb_bias[l]).reshape(n_dec, w_c)
        h_s = _out_proj_s(tail_rows(oa_s), tail_rows(ob_s), tail_rows(oc_s), h_s, out_gain[l], w_out[l], tn)

        i = l // 2
        if is_moe:
            xn_s = _norm(h_s, norm_ffn[l], TAIL)
            h_p, h_s = _moe_layer(xn_p, xn_s, h_p, h_s, moe_router[i], moe_w_gate[i].astype(BF16),
                                  moe_w_up[i].astype(BF16), moe_w_down[i].astype(BF16), n_tok, 512, tf)
        else:
            h_p = _ffn_dense(xn_p, h_p, ffn_w_gate[i].astype(BF16), ffn_w_up[i].astype(BF16),
                             ffn_w_down[i].astype(BF16), tm_big, tf // 2)
            h_s = _ffn_s(h_s, norm_ffn[l], ffn_w_gate[i], ffn_w_up[i], ffn_w_down[i], tf)

        kp_l.append(proj[:, o_k:o_v].reshape(n_batch, seq, h_c, HEAD_DIM))
        vp_l.append(proj[:, o_v:o_v + w_c].reshape(n_batch, seq, h_c, HEAD_DIM))
        pp_l.append(proj[:, 2 * w_a:2 * w_a + w_b].reshape(n_batch, seq, w_b)[:, seq - POOL_BUF:])
        ks_l.append(ps[:, o_k:o_v].reshape(n_dec, 1, h_c, HEAD_DIM))
        vs_l.append(ps[:, o_v:o_v + w_c].reshape(n_dec, 1, h_c, HEAD_DIM))
        ps_l.append(jnp.concatenate([state_pool[l][:, 1:], p_s[:, None, :]], axis=1))
        gs_l.append(va_s.reshape(n_dec, 1, w_a // HEAD_DIM, HEAD_DIM))

    y_p = _norm(h_p, norm_final, TAIL)
    y_s = _norm(h_s, norm_final, TAIL)
    return (y_p.reshape(n_batch, seq, d_model), y_s[:n_dec].reshape(n_dec, 1, d_model),
            jnp.stack(kp_l), jnp.stack(vp_l), jnp.stack(pp_l),
            jnp.stack(ks_l), jnp.stack(vs_l), jnp.stack(ps_l), jnp.stack(gs_l))
```

```python
import functools

import jax
import jax.numpy as jnp
from jax import lax
from jax.experimental import pallas as pl
from jax.experimental.pallas import tpu as pltpu

F32 = jnp.float32
BF16 = jnp.bfloat16

HEAD_DIM = 128
CHUNK = 128
POOL_WINDOWS = (2, 4, 8, 16)
POOL_BUF = max(POOL_WINDOWS) - 1
POOL_HALO = 16
TOP_K = 2
EPS = 1e-6
SB_SCALE = HEAD_DIM ** -0.5
LOG2E = 1.4426950408889634
LANES = 128
TAIL = 256
ATTN_HEADS_PER_STEP = 2
CUMSUM_CHUNK = 256
PAGES_PER_STEP = 16
VMEM_LIMIT = 56 * 1024 * 1024


def _cparams(sem):
    return pltpu.CompilerParams(dimension_semantics=sem, vmem_limit_bytes=VMEM_LIMIT)


def _rms(x, g):
    ms = jnp.mean(x * x, axis=-1, keepdims=True)
    return x * lax.rsqrt(ms + EPS) * g


def _softplus(z):
    return jnp.maximum(z, 0.0) + jnp.log1p(jnp.exp(-jnp.abs(z)))


def _split2(x):
    hi = x.astype(BF16)
    return hi, (x - hi.astype(F32)).astype(BF16)


def _split3(x):
    h1 = x.astype(BF16)
    r1 = x - h1.astype(F32)
    h2 = r1.astype(BF16)
    h3 = (r1 - h2.astype(F32)).astype(BF16)
    return h1, h2, h3


def _dot(a, b):
    return jnp.dot(a, b, preferred_element_type=F32)


def _dot_hp(x, w):
    xh, xl = _split2(x)
    wh, wl = _split2(w)
    return _dot(xh, wh) + (_dot(xl, wh) + _dot(xh, wl))


def _dot3(parts, m):
    return sum(_dot(p, m) for p in parts)


def _row_tile(n_rows, cap):
    best = TAIL
    for k in range(1, n_rows // TAIL + 1):
        if n_rows % (k * TAIL) == 0 and k * TAIL <= cap:
            best = k * TAIL
    return best


def _norm_proj_body(x_ref, g_ref, w_ref, om_ref, ok_ref, ov_ref, xn_ref, *, jk, jv):
    j = pl.program_id(1)

    @pl.when(j == 0)
    def _():
        xn_ref[...] = _rms(x_ref[...], g_ref[...]).astype(BF16)

    res = _dot(xn_ref[...], w_ref[...].astype(BF16))

    @pl.when(j < jk)
    def _():
        om_ref[...] = res

    @pl.when(jnp.logical_and(j >= jk, j < jv))
    def _():
        ok_ref[...] = res

    @pl.when(j >= jv)
    def _():
        ov_ref[...] = res


def _norm_proj(h, g, w, tm, tn, o_k, o_v):
    t, d = h.shape
    n = w.shape[1]
    assert o_k % tn == 0 and o_v % tn == 0 and n % tn == 0
    jk, jv, nj = o_k // tn, o_v // tn, n // tn
    body = functools.partial(_norm_proj_body, jk=jk, jv=jv)
    return pl.pallas_call(
        body,
        out_shape=(jax.ShapeDtypeStruct((t, o_k), F32), jax.ShapeDtypeStruct((t, o_v - o_k), F32),
                   jax.ShapeDtypeStruct((t, n - o_v), F32)),
        grid=(t // tm, nj),
        in_specs=[pl.BlockSpec((tm, d), lambda i, j: (i, 0)),
                  pl.BlockSpec((1, d), lambda i, j: (0, 0)),
                  pl.BlockSpec((d, tn), lambda i, j: (0, j))],
        out_specs=(pl.BlockSpec((tm, tn), lambda i, j: (i, jnp.minimum(j, jk - 1))),
                   pl.BlockSpec((tm, tn), lambda i, j: (i, jnp.clip(j - jk, 0, jv - jk - 1))),
                   pl.BlockSpec((tm, tn), lambda i, j: (i, jnp.clip(j - jv, 0, nj - jv - 1)))),
        scratch_shapes=[pltpu.VMEM((tm, d), BF16)],
        compiler_params=_cparams(("parallel", "arbitrary")),
        name="norm_proj",
    )(h, g.reshape(1, d), w)


def _norm_proj_s_body(x_ref, g_ref, w_ref, o_ref):
    o_ref[...] = _dot_hp(_rms(x_ref[...], g_ref[...]), w_ref[...])


def _norm_proj_s(h, g, w, tn):
    t, d = h.shape
    n = w.shape[1]
    return pl.pallas_call(
        _norm_proj_s_body,
        out_shape=jax.ShapeDtypeStruct((t, n), F32),
        grid=(n // tn,),
        in_specs=[pl.BlockSpec((t, d), lambda j: (0, 0)),
                  pl.BlockSpec((1, d), lambda j: (0, 0)),
                  pl.BlockSpec((d, tn), lambda j: (0, j))],
        out_specs=pl.BlockSpec((t, tn), lambda j: (0, j)),
        compiler_params=_cparams(("parallel",)),
        name="norm_proj_sample",
    )(h, g.reshape(1, d), w)


def _mix_ab_body(ua_ref, va_ref, p_ref, pp_ref, vg_ref, ws_ref, bt_ref, pw_ref, ps_ref,
                 oa_ref, ob_ref, buf_ref, *, rows, n_heads):
    i = pl.program_id(1)
    r_i = lax.broadcasted_iota(jnp.int32, (CHUNK, CHUNK), 0)
    c_i = lax.broadcasted_iota(jnp.int32, (CHUNK, CHUNK), 1)
    causal = r_i >= c_i
    for c in range(rows // CHUNK):
        rs = slice(c * CHUNK, (c + 1) * CHUNK)
        for hd in range(n_heads):
            cs = slice(hd * HEAD_DIM, (hd + 1) * HEAD_DIM)
            u = jax.nn.gelu(ua_ref[rs, cs])
            van = _rms(jax.nn.gelu(va_ref[rs, cs]), vg_ref[:, cs])
            w = jnp.where(causal, ws_ref[hd], 0.0).astype(BF16)
            mixed = _dot(w, van.astype(BF16)) + bt_ref[:, hd:hd + 1]
            oa_ref[rs, cs] = u * mixed

    buf_ref[0:POOL_HALO, :] = jnp.where(i == 0, 0.0, pp_ref[...])
    buf_ref[POOL_HALO:POOL_HALO + rows, :] = p_ref[...]
    pos = i * rows + lax.broadcasted_iota(jnp.int32, (rows, 1), 0)
    for g, w in enumerate(POOL_WINDOWS):
        cs = slice(g * LANES, (g + 1) * LANES)
        s = buf_ref[POOL_HALO:POOL_HALO + rows, cs]
        for j in range(1, w):
            s = s + buf_ref[POOL_HALO - j:POOL_HALO - j + rows, cs]
        cnt = jnp.minimum(pos + 1, w).astype(F32)
        d = s / cnt - p_ref[:, cs]
        ob_ref[:, cs] = _dot(d.astype(BF16), pw_ref[g].astype(BF16)) * ps_ref[:, cs]


def _mix_ab_prompt(proj, n_batch, seq, v_gain, w_s, b_t, pool_w, pool_scale, w_a, w_b):
    rows = 256 if seq % 256 == 0 else CHUNK
    nr = seq // rows
    n_heads = w_a // HEAD_DIM
    halo_per_block = rows // POOL_HALO
    ca, cv, cp = 0, 1, 2 * w_a // w_b
    body = functools.partial(_mix_ab_body, rows=rows, n_heads=n_heads)
    return pl.pallas_call(
        body,
        out_shape=(jax.ShapeDtypeStruct((n_batch * seq, w_a), F32),
                   jax.ShapeDtypeStruct((n_batch * seq, w_b), F32)),
        grid=(n_batch, nr),
        in_specs=[pl.BlockSpec((rows, w_a), lambda b, i: (b * nr + i, ca)),
                  pl.BlockSpec((rows, w_a), lambda b, i: (b * nr + i, cv)),
                  pl.BlockSpec((rows, w_b), lambda b, i: (b * nr + i, cp)),
                  pl.BlockSpec((POOL_HALO, w_b),
                               lambda b, i: (jnp.maximum((b * nr + i) * halo_per_block - 1, 0), cp)),
                  pl.BlockSpec((1, w_a), lambda b, i: (0, 0)),
                  pl.BlockSpec(w_s.shape, lambda b, i: (0, 0, 0)),
                  pl.BlockSpec(b_t.shape, lambda b, i: (0, 0)),
                  pl.BlockSpec(pool_w.shape, lambda b, i: (0, 0, 0)),
                  pl.BlockSpec((1, w_b), lambda b, i: (0, 0))],
        out_specs=(pl.BlockSpec((rows, w_a), lambda b, i: (b * nr + i, 0)),
                   pl.BlockSpec((rows, w_b), lambda b, i: (b * nr + i, 0))),
        scratch_shapes=[pltpu.VMEM((POOL_HALO + rows, w_b), F32)],
        compiler_params=_cparams(("parallel", "arbitrary")),
        name="mix_ab_prompt",
    )(proj, proj, proj, proj, v_gain.reshape(1, w_a), w_s, b_t, pool_w, pool_scale.reshape(1, w_b))


def _mix_ab_sample_body(ua_ref, va_ref, p_ref, st_ref, vg_ref, w0_ref, b0_ref, pw_ref, ps_ref,
                        oa_ref, ob_ref, vn_ref, *, n_heads, past_len):
    for hd in range(n_heads):
        cs = slice(hd * HEAD_DIM, (hd + 1) * HEAD_DIM)
        u = jax.nn.gelu(ua_ref[:, cs])
        van = _rms(jax.nn.gelu(va_ref[:, cs]), vg_ref[:, cs])
        vn_ref[:, cs] = van
        oa_ref[:, cs] = u * (w0_ref[:, cs] * van + b0_ref[:, cs])
    for g, w in enumerate(POOL_WINDOWS):
        cs = slice(g * LANES, (g + 1) * LANES)
        s = p_ref[:, cs]
        for j in range(1, w):
            s = s + st_ref[:, POOL_BUF - j, cs]
        cnt = float(min(past_len + 1, w))
        d = s / cnt - p_ref[:, cs]
        ob_ref[:, cs] = _dot_hp(d, pw_ref[g]) * ps_ref[:, cs]


def _mix_ab_sample(ua, va, p, state, v_gain, w0, b0, pool_w, pool_scale, past_len):
    n, w_a = ua.shape
    w_b = p.shape[1]
    body = functools.partial(_mix_ab_sample_body, n_heads=w_a // HEAD_DIM, past_len=past_len)
    return pl.pallas_call(
        body,
        out_shape=(jax.ShapeDtypeStruct((n, w_a), F32), jax.ShapeDtypeStruct((n, w_b), F32),
                   jax.ShapeDtypeStruct((n, w_a), F32)),
        name="mix_ab_sample",
    )(ua, va, p, state, v_gain.reshape(1, w_a), w0, b0, pool_w, pool_scale.reshape(1, w_b))


def _attn_body(bias_ref, q_ref, k_ref, v_ref, o_ref, acc_ref, *car_refs, tq, nh):
    hg = pl.program_id(1)
    qi = pl.program_id(2)
    ch = CUMSUM_CHUNK
    n_ch = tq // ch
    r_i = lax.broadcasted_iota(jnp.int32, (2 * ch, ch), 0)
    c_i = lax.broadcasted_iota(jnp.int32, (2 * ch, ch), 1)
    later2 = (jnp.where(r_i >= ch, r_i - ch, r_i) > c_i).astype(BF16)
    visible = lax.broadcasted_iota(jnp.int32, (tq, tq), 1) < lax.broadcasted_iota(jnp.int32, (tq, tq), 0)
    qs = [q_ref[:, hh * HEAD_DIM:(hh + 1) * HEAD_DIM].astype(BF16) for hh in range(nh)]
    bias2 = [bias_ref[hg * nh + hh] * LOG2E for hh in range(nh)]

    def block(kj, masked):
        off = pl.multiple_of(kj * tq, tq)
        for hh in range(nh):
            cs = slice(hh * HEAD_DIM, (hh + 1) * HEAD_DIM)
            kb = k_ref[pl.ds(off, tq), cs].astype(BF16)
            vb = v_ref[pl.ds(off, tq), cs].astype(BF16)
            z2 = lax.dot_general(qs[hh], kb, (((1,), (1,)), ((), ())),
                                 preferred_element_type=F32) * (SB_SCALE * LOG2E) + bias2[hh]
            sp2 = jnp.maximum(z2, 0.0) + jnp.log2(1.0 + jnp.exp2(-jnp.abs(z2)))
            log2_b = z2 - sp2
            if masked:
                sp2 = jnp.where(visible, sp2, 0.0)
            hi, lo = _split2(sp2)
            stacked = jnp.concatenate(
                [jnp.concatenate([hi[:, c * ch:(c + 1) * ch], lo[:, c * ch:(c + 1) * ch]], axis=1)
                 for c in range(n_ch)], axis=0)
            within = _dot(stacked, later2)
            run = car_refs[hh][...]
            after = [None] * n_ch
            for c in reversed(range(n_ch)):
                after[c] = within[c * tq:(c + 1) * tq, :] + run
                run = run + jnp.sum(sp2[:, c * ch:(c + 1) * ch], axis=1, keepdims=True)
            car_refs[hh][...] = run
            a = jnp.exp2(log2_b - jnp.concatenate(after, axis=1))
            if masked:
                a = jnp.where(visible, a, 0.0)
            acc_ref[:, cs] += _dot(a.astype(BF16), vb)

    acc_ref[...] = jnp.zeros_like(acc_ref)
    for car_ref in car_refs:
        car_ref[...] = jnp.zeros_like(car_ref)
    block(qi, True)

    def body(n, carry):
        block(qi - 1 - n, False)
        return carry

    lax.fori_loop(0, qi, body, 0)
    o_ref[...] = acc_ref[...]


def _attn_prompt(proj, k_all, v_all, sb_bias, n_batch, seq, n_heads, col_q):
    tq = next(t for t in (512, 256, 128) if seq % t == 0)
    nq = seq // tq
    nh = ATTN_HEADS_PER_STEP
    assert n_heads % nh == 0 and col_q % nh == 0
    wq = nh * HEAD_DIM
    body = functools.partial(_attn_body, tq=tq, nh=nh)
    return pl.pallas_call(
        body,
        out_shape=jax.ShapeDtypeStruct((n_batch * seq, n_heads * HEAD_DIM), F32),
        grid_spec=pltpu.PrefetchScalarGridSpec(
            num_scalar_prefetch=1,
            grid=(n_batch, n_heads // nh, nq),
            in_specs=[pl.BlockSpec((tq, wq), lambda b, h, i, s: (b * nq + i, col_q // nh + h)),
                      pl.BlockSpec((seq, wq), lambda b, h, i, s: (b, h)),
                      pl.BlockSpec((seq, wq), lambda b, h, i, s: (b, h))],
            out_specs=pl.BlockSpec((tq, wq), lambda b, h, i, s: (b * nq + i, h)),
            scratch_shapes=[pltpu.VMEM((tq, wq), F32)] + [pltpu.VMEM((tq, 1), F32)] * nh),
        compiler_params=_cparams(("parallel", "parallel", "arbitrary")),
        name="attn_prompt",
    )(sb_bias, proj, k_all, v_all)


def _paged_body(pt_ref, q_ref, bias_ref, later_ref, same_ref, *rest, n_groups, n_heads):
    del pt_ref
    pps = PAGES_PER_STEP
    k_refs = rest[:pps]
    v_refs = rest[pps:2 * pps]
    o_ref, z_ref, a_ref, acc_ref = rest[2 * pps:]
    s = pl.program_id(1)
    page_rows = k_refs[0].shape[1] * n_heads
    lane = lax.broadcasted_iota(jnp.int32, (2 * n_heads, page_rows), 1)
    sub = lax.broadcasted_iota(jnp.int32, (2 * n_heads, page_rows), 0)
    own_head = (lane % n_heads) == (sub % n_heads)
    nt = (((1,), (1,)), ((), ()))

    @pl.when(s < n_groups)
    def _():
        qb = jnp.concatenate(_split2(q_ref[0]), axis=0)
        for i in range(pps):
            kh, kl = _split2(k_refs[i][0].reshape(page_rows, HEAD_DIM))
            s16 = (lax.dot_general(qb, kh, nt, preferred_element_type=F32)
                   + lax.dot_general(qb, kl, nt, preferred_element_type=F32))
            z_ref[pl.ds(s * pps + i, 1), :] = jnp.sum(jnp.where(own_head, s16, 0.0), axis=0, keepdims=True)

    @pl.when(s == n_groups - 1)
    def _():
        n_pages = z_ref.shape[0]
        z = z_ref[...] * SB_SCALE + bias_ref[...]
        sp = _softplus(z)
        log_1mb = -sp
        log_b = z - sp
        parts = _split3(log_1mb)
        within = _dot3(parts, later_ref[...])
        totals = _dot3(parts, same_ref[...])
        p_r = lax.broadcasted_iota(jnp.int32, (n_pages, n_pages), 0)
        p_c = lax.broadcasted_iota(jnp.int32, (n_pages, n_pages), 1)
        later_page = (p_c > p_r).astype(BF16)
        carry = sum(_dot(later_page, t) for t in _split3(totals))
        a_ref[...] = jnp.exp(log_b + within + carry)
        acc_ref[...] = jnp.zeros_like(acc_ref)

    @pl.when(s >= n_groups)
    def _():
        acc = acc_ref[...]
        for i in range(pps):
            ah, al = _split2(a_ref[pl.ds((s - n_groups) * pps + i, 1), :])
            rows = jnp.concatenate([jnp.broadcast_to(ah, (n_heads, page_rows)),
                                    jnp.broadcast_to(al, (n_heads, page_rows))], axis=0)
            am = jnp.where(own_head, rows, jnp.zeros_like(rows))
            vh, vl = _split2(v_refs[i][0].reshape(page_rows, HEAD_DIM))
            acc = acc + (_dot(am, vh) + _dot(am, vl))
        acc_ref[...] = acc

    @pl.when(s == 2 * n_groups - 1)
    def _():
        o_ref[0] = acc_ref[0:n_heads, :] + acc_ref[n_heads:, :]


def _attn_sample(q, cache_k, cache_v, layer, page_table, sb_bias):
    n_dec, n_heads, _ = q.shape
    n_pages = page_table.shape[1]
    page = cache_k.shape[2]
    pps = PAGES_PER_STEP
    assert n_pages % pps == 0
    n_groups = n_pages // pps
    page_rows = page * n_heads
    idx = jnp.arange(page_rows, dtype=jnp.int32)
    same_head = (idx[:, None] % n_heads) == (idx[None, :] % n_heads)
    later = (same_head & (idx[:, None] // n_heads > idx[None, :] // n_heads)).astype(BF16)
    same = same_head.astype(BF16)
    bias_lane = jnp.tile(sb_bias.astype(F32), page).reshape(1, page_rows)

    def k_map(i):
        return lambda b, s, pt: (layer, pt[b, jnp.minimum(s, n_groups - 1) * pps + i], 0, 0, 0)

    def v_map(i):
        return lambda b, s, pt: (layer, pt[b, jnp.maximum(s - n_groups, 0) * pps + i], 0, 0, 0)

    page_block = (None, 1, page, n_heads, HEAD_DIM)
    body = functools.partial(_paged_body, n_groups=n_groups, n_heads=n_heads)
    return pl.pallas_call(
        body,
        out_shape=jax.ShapeDtypeStruct((n_dec, n_heads, HEAD_DIM), F32),
        grid_spec=pltpu.PrefetchScalarGridSpec(
            num_scalar_prefetch=1,
            grid=(n_dec, 2 * n_groups),
            in_specs=([pl.BlockSpec((1, n_heads, HEAD_DIM), lambda b, s, pt: (b, 0, 0)),
                       pl.BlockSpec((1, page_rows), lambda b, s, pt: (0, 0)),
                       pl.BlockSpec((page_rows, page_rows), lambda b, s, pt: (0, 0)),
                       pl.BlockSpec((page_rows, page_rows), lambda b, s, pt: (0, 0))]
                      + [pl.BlockSpec(page_block, k_map(i)) for i in range(pps)]
                      + [pl.BlockSpec(page_block, v_map(i)) for i in range(pps)]),
            out_specs=pl.BlockSpec((1, n_heads, HEAD_DIM), lambda b, s, pt: (b, 0, 0)),
            scratch_shapes=[pltpu.VMEM((n_pages, page_rows), F32),
                            pltpu.VMEM((n_pages, page_rows), F32),
                            pltpu.VMEM((2 * n_heads, HEAD_DIM), F32)]),
        compiler_params=_cparams(("arbitrary", "arbitrary")),
        name="attn_sample",
    )(page_table, q, bias_lane, later, same, *([cache_k] * pps), *([cache_v] * pps))


def _merged(oa_ref, ob_ref, oc_ref, g_ref, w_a, w_b):
    return (_rms(oa_ref[...], g_ref[:, 0:w_a]), _rms(ob_ref[...], g_ref[:, w_a:w_a + w_b]),
            _rms(oc_ref[...], g_ref[:, w_a + w_b:]))


def _out_proj_body(oa_ref, ob_ref, oc_ref, h_ref, g_ref, w_ref, gn_ref, ho_ref, xn_ref, *, w_a, w_b):
    a, b, c = _merged(oa_ref, ob_ref, oc_ref, g_ref, w_a, w_b)
    acc = _dot(a.astype(BF16), w_ref[0:w_a, :])
    acc = acc + _dot(b.astype(BF16), w_ref[w_a:w_a + w_b, :])
    acc = acc + _dot(c.astype(BF16), w_ref[w_a + w_b:, :])
    hn = h_ref[...] + acc
    ho_ref[...] = hn
    xn_ref[...] = _rms(hn, gn_ref[...]).astype(xn_ref.dtype)


def _out_proj(o_a, o_b, o_c, h, out_gain, w_out, norm_next, xn_dtype, tm):
    t, d = h.shape
    w_a, w_b, w_c = o_a.shape[1], o_b.shape[1], o_c.shape[1]
    d_mix = w_a + w_b + w_c
    body = functools.partial(_out_proj_body, w_a=w_a, w_b=w_b)
    row = lambda i: (i, 0)
    fixed = lambda i: (0, 0)
    return pl.pallas_call(
        body,
        out_shape=(jax.ShapeDtypeStruct((t, d), F32), jax.ShapeDtypeStruct((t, d), xn_dtype)),
        grid=(t // tm,),
        in_specs=[pl.BlockSpec((tm, w_a), row), pl.BlockSpec((tm, w_b), row), pl.BlockSpec((tm, w_c), row),
                  pl.BlockSpec((tm, d), row), pl.BlockSpec((1, d_mix), fixed),
                  pl.BlockSpec((d_mix, d), fixed), pl.BlockSpec((1, d), fixed)],
        out_specs=(pl.BlockSpec((tm, d), row), pl.BlockSpec((tm, d), row)),
        compiler_params=_cparams(("parallel",)),
        name="out_proj",
    )(o_a, o_b, o_c, h, out_gain.reshape(1, d_mix), w_out, norm_next.reshape(1, d))


def _out_proj_s_body(oa_ref, ob_ref, oc_ref, h_ref, g_ref, w_ref, ho_ref, *, w_a, w_b):
    merged = jnp.concatenate(_merged(oa_ref, ob_ref, oc_ref, g_ref, w_a, w_b), axis=1)
    ho_ref[...] = h_ref[...] + _dot_hp(merged, w_ref[...])


def _out_proj_s(o_a, o_b, o_c, h, out_gain, w_out, tn):
    t, d = h.shape
    w_a, w_b, w_c = o_a.shape[1], o_b.shape[1], o_c.shape[1]
    d_mix = w_a + w_b + w_c
    body = functools.partial(_out_proj_s_body, w_a=w_a, w_b=w_b)
    fixed = lambda j: (0, 0)
    col = lambda j: (0, j)
    return pl.pallas_call(
        body,
        out_shape=jax.ShapeDtypeStruct((t, d), F32),
        grid=(d // tn,),
        in_specs=[pl.BlockSpec((t, w_a), fixed), pl.BlockSpec((t, w_b), fixed), pl.BlockSpec((t, w_c), fixed),
                  pl.BlockSpec((t, tn), col), pl.BlockSpec((1, d_mix), fixed), pl.BlockSpec((d_mix, tn), col)],
        out_specs=pl.BlockSpec((t, tn), col),
        compiler_params=_cparams(("parallel",)),
        name="out_proj_sample",
    )(o_a, o_b, o_c, h, out_gain.reshape(1, d_mix), w_out)


def _ffn_body(x_ref, h_ref, wg_ref, wu_ref, wd_ref, o_ref):
    @pl.when(pl.program_id(1) == 0)
    def _():
        o_ref[...] = h_ref[...]

    x = x_ref[...]
    hid = (jax.nn.silu(_dot(x, wg_ref[...].astype(BF16))) * _dot(x, wu_ref[...].astype(BF16))).astype(BF16)
    o_ref[...] += _dot(hid, wd_ref[...].astype(BF16))


def _ffn_dense(xn, h, w_gate, w_up, w_down, tm, tf):
    t, d = h.shape
    d_ff = w_gate.shape[1]
    return pl.pallas_call(
        _ffn_body,
        out_shape=jax.ShapeDtypeStruct((t, d), F32),
        grid=(t // tm, d_ff // tf),
        in_specs=[pl.BlockSpec((tm, d), lambda i, f: (i, 0)),
                  pl.BlockSpec((tm, d), lambda i, f: (i, 0)),
                  pl.BlockSpec((d, tf), lambda i, f: (0, f)),
                  pl.BlockSpec((d, tf), lambda i, f: (0, f)),
                  pl.BlockSpec((tf, d), lambda i, f: (f, 0))],
        out_specs=pl.BlockSpec((tm, d), lambda i, f: (i, 0)),
        compiler_params=_cparams(("parallel", "arbitrary")),
        name="ffn_dense",
    )(xn, h, w_gate, w_up, w_down)


def _ffn_s_body(h_ref, gn_ref, wg_ref, wu_ref, wd_ref, o_ref):
    @pl.when(pl.program_id(0) == 0)
    def _():
        o_ref[...] = h_ref[...]

    xn = _rms(h_ref[...], gn_ref[...])
    hid = jax.nn.silu(_dot_hp(xn, wg_ref[...])) * _dot_hp(xn, wu_ref[...])
    o_ref[...] += _dot_hp(hid, wd_ref[...])


def _ffn_s(h, norm, w_gate, w_up, w_down, tf):
    t, d = h.shape
    d_ff = w_gate.shape[1]
    fixed = lambda f: (0, 0)
    return pl.pallas_call(
        _ffn_s_body,
        out_shape=jax.ShapeDtypeStruct((t, d), F32),
        grid=(d_ff // tf,),
        in_specs=[pl.BlockSpec((t, d), fixed), pl.BlockSpec((1, d), fixed),
                  pl.BlockSpec((d, tf), lambda f: (0, f)),
                  pl.BlockSpec((d, tf), lambda f: (0, f)),
                  pl.BlockSpec((tf, d), lambda f: (f, 0))],
        out_specs=pl.BlockSpec((t, d), fixed),
        compiler_params=_cparams(("arbitrary",)),
        name="ffn_sample",
    )(h, norm.reshape(1, d), w_gate, w_up, w_down)


def _router_body(xp_ref, xs_ref, rt_ref, meta_ref, cnt_ref, carry_ref, *, tm, n_exp, n_tok):
    i = pl.program_id(0)

    @pl.when(i == 0)
    def _():
        carry_ref[...] = jnp.zeros_like(carry_ref)

    x = jnp.where(i == pl.num_programs(0) - 1, xs_ref[...], xp_ref[...])
    lane = lax.broadcasted_iota(jnp.int32, (tm, LANES), 1)
    neg = jnp.float32(-jnp.inf)
    logits = jnp.full((tm, LANES), neg, F32)
    for e in range(n_exp):
        s = jnp.sum(x * rt_ref[e:e + 1, :], axis=-1, keepdims=True)
        logits = jnp.where(lane == e, s, logits)
    m1 = jnp.max(logits, axis=-1, keepdims=True)
    i1 = jnp.min(jnp.where(logits == m1, lane, LANES), axis=-1, keepdims=True)
    rest = jnp.where(lane == i1, neg, logits)
    m2 = jnp.max(rest, axis=-1, keepdims=True)
    i2 = jnp.min(jnp.where(rest == m2, lane, LANES), axis=-1, keepdims=True)
    e2 = jnp.exp(m2 - m1)
    den = 1.0 + e2
    g1 = 1.0 / den
    g2 = e2 / den
    row = i * tm + lax.broadcasted_iota(jnp.int32, (tm, 1), 0)
    chosen = ((lane == i1) | (lane == i2)) & (row < n_tok)
    onehot = jnp.where(chosen, 1.0, 0.0)
    r_i = lax.broadcasted_iota(jnp.int32, (tm, tm), 0)
    c_i = lax.broadcasted_iota(jnp.int32, (tm, tm), 1)
    earlier = (c_i < r_i).astype(BF16)
    before = _dot(earlier, onehot.astype(BF16)) + carry_ref[...]
    r1 = jnp.sum(jnp.where(lane == i1, before, 0.0), axis=-1, keepdims=True)
    r2 = jnp.sum(jnp.where(lane == i2, before, 0.0), axis=-1, keepdims=True)
    carry_ref[...] += jnp.sum(onehot, axis=0, keepdims=True)
    meta = jnp.zeros((tm, LANES), F32)
    for k, val in enumerate((i1.astype(F32), i2.astype(F32), g1, g2, r1, r2)):
        meta = jnp.where(lane == k, val, meta)
    meta_ref[...] = meta
    cnt_ref[...] = carry_ref[...]


def _router(xn_p, xn_s, router_t, n_tok):
    tm = TAIL
    n_p, d = xn_p.shape
    n_exp = router_t.shape[0]
    n_blocks = n_p // tm + 1
    body = functools.partial(_router_body, tm=tm, n_exp=n_exp, n_tok=n_tok)
    return pl.pallas_call(
        body,
        out_shape=(jax.ShapeDtypeStruct((n_blocks * tm, LANES), F32), jax.ShapeDtypeStruct((1, LANES), F32)),
        grid=(n_blocks,),
        in_specs=[pl.BlockSpec((tm, d), lambda i: (jnp.minimum(i, n_blocks - 2), 0)),
                  pl.BlockSpec((tm, d), lambda i: (0, 0)),
                  pl.BlockSpec((n_exp, d), lambda i: (0, 0))],
        out_specs=(pl.BlockSpec((tm, LANES), lambda i: (i, 0)), pl.BlockSpec((1, LANES), lambda i: (0, 0))),
        scratch_shapes=[pltpu.VMEM((1, LANES), F32)],
        compiler_params=_cparams(("arbitrary",)),
        name="moe_router",
    )(xn_p, xn_s, router_t)


def _dispatch_body(tok_ref, nb_ref, xp_hbm, xs_hbm, o_ref, buf, sems, *, tm, n_p):
    b = pl.program_id(0)
    n_used = nb_ref[0]
    slot = b % 2

    def issue_block(blk, sl):
        def issue(r, carry):
            tok = tok_ref[blk * tm + r]
            dst = buf.at[sl, pl.ds(r, 1), :]

            @pl.when(tok < n_p)
            def _():
                pltpu.make_async_copy(xp_hbm.at[pl.ds(tok, 1), :], dst, sems.at[sl]).start()

            @pl.when(tok >= n_p)
            def _():
                pltpu.make_async_copy(xs_hbm.at[pl.ds(tok - n_p, 1), :], dst, sems.at[sl]).start()

            return carry

        lax.fori_loop(0, tm, issue, 0)

    def drain(r, carry):
        pltpu.make_async_copy(xs_hbm.at[pl.ds(0, 1), :], buf.at[slot, pl.ds(r, 1), :], sems.at[slot]).wait()
        return carry

    @pl.when(b == 0)
    def _():
        issue_block(0, 0)

    @pl.when(b + 1 < n_used)
    def _():
        issue_block(b + 1, 1 - slot)

    @pl.when(b < n_used)
    def _():
        lax.fori_loop(0, tm, drain, 0)
        o_ref[...] = buf[slot].astype(BF16)

    @pl.when(b >= n_used)
    def _():
        o_ref[...] = jnp.zeros_like(o_ref)


def _dispatch(slot_tok, nb_used, xn_p, xn_s, tm):
    n_slots = slot_tok.shape[0]
    n_p, d = xn_p.shape
    body = functools.partial(_dispatch_body, tm=tm, n_p=n_p)
    return pl.pallas_call(
        body,
        out_shape=jax.ShapeDtypeStruct((n_slots, d), BF16),
        grid_spec=pltpu.PrefetchScalarGridSpec(
            num_scalar_prefetch=2,
            grid=(n_slots // tm,),
            in_specs=[pl.BlockSpec(memory_space=pl.ANY), pl.BlockSpec(memory_space=pl.ANY)],
            out_specs=pl.BlockSpec((tm, d), lambda b, st, nu: (b, 0)),
            scratch_shapes=[pltpu.VMEM((2, tm, d), F32), pltpu.SemaphoreType.DMA((2,))]),
        compiler_params=_cparams(("arbitrary",)),
        name="moe_dispatch",
    )(slot_tok, nb_used, xn_p, xn_s)


def _moe_body(be_ref, nb_ref, x_ref, wg_ref, wu_ref, wd_ref, o_ref):
    del be_ref
    b = pl.program_id(0)
    f = pl.program_id(1)

    @pl.when(f == 0)
    def _():
        o_ref[...] = jnp.zeros_like(o_ref)

    @pl.when(b < nb_ref[0])
    def _():
        xb = x_ref[...]
        hid = (jax.nn.silu(_dot(xb, wg_ref[...])) * _dot(xb, wu_ref[...])).astype(BF16)
        o_ref[...] += _dot(hid, wd_ref[...])


def _moe_experts(blk_expert, nb_used, slots, w_gate, w_up, w_down, tm, tf):
    n_slots, d = slots.shape
    d_ff = w_gate.shape[2]
    nf = d_ff // tf
    nb = n_slots // tm

    def w_in_map(b, f, be, nu):
        live = b < nu[0]
        return be[jnp.minimum(b, nu[0] - 1)], 0, jnp.where(live, f, nf - 1)

    def w_out_map(b, f, be, nu):
        live = b < nu[0]
        return be[jnp.minimum(b, nu[0] - 1)], jnp.where(live, f, nf - 1), 0

    return pl.pallas_call(
        _moe_body,
        out_shape=jax.ShapeDtypeStruct((n_slots, d), F32),
        grid_spec=pltpu.PrefetchScalarGridSpec(
            num_scalar_prefetch=2,
            grid=(nb, nf),
            in_specs=[pl.BlockSpec((tm, d), lambda b, f, be, nu: (b, 0)),
                      pl.BlockSpec((None, d, tf), w_in_map),
                      pl.BlockSpec((None, d, tf), w_in_map),
                      pl.BlockSpec((None, tf, d), w_out_map)],
            out_specs=pl.BlockSpec((tm, d), lambda b, f, be, nu: (b, 0))),
        compiler_params=_cparams(("arbitrary", "arbitrary")),
        name="moe_experts",
    )(blk_expert, nb_used, slots, w_gate, w_up, w_down)


def _combine_body(dest_ref, y_hbm, hp_ref, hs_ref, meta_ref, op_ref, os_ref, buf, sems, *, tm, n_tok):
    i = pl.program_id(0)
    n_blocks = pl.num_programs(0)
    is_tail = i == n_blocks - 1
    slot = i % 2

    def row_copy(blk, sl, r, k):
        t = jnp.minimum(blk * tm + r, n_tok - 1)
        return pltpu.make_async_copy(y_hbm.at[pl.ds(dest_ref[t * TOP_K + k], 1), :],
                                     buf.at[sl, k, pl.ds(r, 1), :], sems.at[sl])

    def issue_block(blk, sl):
        def issue(r, carry):
            for k in range(TOP_K):
                row_copy(blk, sl, r, k).start()
            return carry

        lax.fori_loop(0, tm, issue, 0)

    def drain(r, carry):
        for k in range(TOP_K):
            row_copy(i, slot, r, k).wait()
        return carry

    @pl.when(i == 0)
    def _():
        issue_block(0, 0)

    @pl.when(i + 1 < n_blocks)
    def _():
        issue_block(i + 1, 1 - slot)

    lax.fori_loop(0, tm, drain, 0)
    valid = (i * tm + lax.broadcasted_iota(jnp.int32, (tm, 1), 0)) < n_tok
    g1 = jnp.where(valid, meta_ref[:, 2:3], 0.0)
    g2 = jnp.where(valid, meta_ref[:, 3:4], 0.0)
    y = buf[slot, 0] * g1 + buf[slot, 1] * g2

    @pl.when(is_tail)
    def _():
        os_ref[...] = hs_ref[...] + y

    @pl.when(jnp.logical_not(is_tail))
    def _():
        op_ref[...] = hp_ref[...] + y


def _combine(dest, y_slots, h_p, h_s, meta, n_tok):
    tm = TAIL
    n_p, d = h_p.shape
    n_blocks = n_p // tm + 1
    body = functools.partial(_combine_body, tm=tm, n_tok=n_tok)
    return pl.pallas_call(
        body,
        out_shape=(jax.ShapeDtypeStruct((n_p, d), F32), jax.ShapeDtypeStruct((tm, d), F32)),
        grid_spec=pltpu.PrefetchScalarGridSpec(
            num_scalar_prefetch=1,
            grid=(n_blocks,),
            in_specs=[pl.BlockSpec(memory_space=pl.ANY),
                      pl.BlockSpec((tm, d), lambda i, ds: (jnp.minimum(i, n_blocks - 2), 0)),
                      pl.BlockSpec((tm, d), lambda i, ds: (0, 0)),
                      pl.BlockSpec((tm, LANES), lambda i, ds: (i, 0))],
            out_specs=(pl.BlockSpec((tm, d), lambda i, ds: (jnp.minimum(i, n_blocks - 2), 0)),
                       pl.BlockSpec((tm, d), lambda i, ds: (0, 0))),
            scratch_shapes=[pltpu.VMEM((2, TOP_K, tm, d), F32), pltpu.SemaphoreType.DMA((2,))]),
        compiler_params=_cparams(("arbitrary",)),
        name="moe_combine",
    )(dest, y_slots, h_p, h_s, meta)


def _moe_layer(xn_p, xn_s, h_p, h_s, router, w_gate, w_up, w_down, n_tok, tm_e, tf):
    n_exp = router.shape[1]
    meta, cnt = _router(xn_p, xn_s, jnp.transpose(router), n_tok)
    expert = meta[:n_tok, 0:TOP_K].astype(jnp.int32)
    rank = meta[:n_tok, 4:4 + TOP_K].astype(jnp.int32)
    counts = cnt[0, :n_exp].astype(jnp.int32)
    padded = (counts + tm_e - 1) // tm_e * tm_e
    ends = jnp.cumsum(padded)
    starts = ends - padded
    dest = (starts[expert] + rank).reshape(-1)
    n_assign = n_tok * TOP_K
    nb = -(-(n_assign + n_exp * (tm_e - 1)) // tm_e)
    blk_expert = jnp.minimum(
        jnp.searchsorted(ends, jnp.arange(nb, dtype=jnp.int32) * tm_e, side='right'), n_exp - 1).astype(jnp.int32)
    nb_used = (ends[-1:] // tm_e).astype(jnp.int32)
    tok = jnp.repeat(jnp.arange(n_tok, dtype=jnp.int32), TOP_K)
    slot_tok = jnp.full((nb * tm_e,), n_tok, jnp.int32).at[dest].set(tok)
    slots = _dispatch(slot_tok, nb_used, xn_p, xn_s, tm_e)
    y_slots = _moe_experts(blk_expert, nb_used, slots, w_gate, w_up, w_down, tm_e, tf)
    return _combine(dest, y_slots, h_p, h_s, meta, n_tok)


def _norm_body(x_ref, g_ref, o_ref):
    o_ref[...] = _rms(x_ref[...], g_ref[...])


def _norm(h, g, tm):
    t, d = h.shape
    return pl.pallas_call(
        _norm_body,
        out_shape=jax.ShapeDtypeStruct((t, d), F32),
        grid=(t // tm,),
        in_specs=[pl.BlockSpec((tm, d), lambda i: (i, 0)), pl.BlockSpec((1, d), lambda i: (0, 0))],
        out_specs=pl.BlockSpec((tm, d), lambda i: (i, 0)),
        compiler_params=_cparams(("parallel",)),
        name="row_norm",
    )(h, g.reshape(1, d))


def kernel(x_prompt, x_sample, cache_k, cache_v, state_pool, page_table, norm_mix, w_in, gmlp_v_gain, gmlp_w,
           gmlp_b, pool_w, pool_scale, sb_bias, out_gain, w_out, norm_ffn, ffn_w_gate, ffn_w_up, ffn_w_down,
           moe_router, moe_w_gate, moe_w_up, moe_w_down, norm_final):
    n_batch, seq, d_model = x_prompt.shape
    n_dec, dec_seq, _ = x_sample.shape
    depth = norm_mix.shape[0]
    page = cache_k.shape[2]
    h_c = cache_k.shape[3]
    past_len = page_table.shape[1] * page
    w_a = gmlp_v_gain.shape[1] * HEAD_DIM
    w_b = pool_scale.shape[1]
    w_c = h_c * HEAD_DIM
    d_ff = ffn_w_gate.shape[2]
    assert dec_seq == 1 and n_dec < TAIL and seq % CHUNK == 0 and (n_batch * seq) % TAIL == 0
    assert past_len % CHUNK == 0 and w_a == w_b

    n_p = n_batch * seq
    n_tok = n_p + n_dec
    col_q = (2 * w_a + w_b) // HEAD_DIM
    col_k = col_q + h_c
    col_v = col_k + h_c
    o_q, o_k, o_v = col_q * HEAD_DIM, col_k * HEAD_DIM, col_v * HEAD_DIM
    tm_big = _row_tile(n_p, 1024)
    tf = 512 if d_ff % 512 == 0 else 256
    tn = 512

    def tail_rows(x):
        return jnp.pad(x, ((0, TAIL - n_dec), (0, 0)))

    h_p = x_prompt.reshape(n_p, d_model)
    h_s = tail_rows(x_sample.reshape(n_dec, d_model))

    kp_l, vp_l, pp_l, ks_l, vs_l, ps_l, gs_l = [], [], [], [], [], [], []
    for l in range(depth):
        proj, k_p, v_p = _norm_proj(h_p, norm_mix[l], w_in[l], tm_big, tn, o_k, o_v)
        o_a, o_b = _mix_ab_prompt(proj, n_batch, seq, gmlp_v_gain[l], gmlp_w[l],
                                  jnp.transpose(gmlp_b[l]), pool_w[l], pool_scale[l], w_a, w_b)
        o_c = _attn_prompt(proj, k_p, v_p, sb_bias[l], n_batch, seq, h_c, col_q)
        is_moe = l % 2 == 1
        h_p, xn_p = _out_proj(o_a, o_b, o_c, h_p, out_gain[l], w_out[l].astype(BF16), norm_ffn[l],
                              F32 if is_moe else BF16, TAIL)

        ps = _norm_proj_s(h_s, norm_mix[l], w_in[l], tn)[:n_dec]
        p_s = ps[:, 2 * w_a:2 * w_a + w_b]
        w0 = jnp.repeat(gmlp_w[l, :, 0, 0], HEAD_DIM).reshape(1, w_a)
        b0 = jnp.repeat(gmlp_b[l, :, 0], HEAD_DIM).reshape(1, w_a)
        oa_s, ob_s, va_s = _mix_ab_sample(ps[:, 0:w_a], ps[:, w_a:2 * w_a], p_s, state_pool[l],
                                          gmlp_v_gain[l], w0, b0, pool_w[l], pool_scale[l], past_len)
        q_s = ps[:, o_q:o_k].reshape(n_dec, h_c, HEAD_DIM)
        oc_s = _attn_sample(q_s, cache_k, cache_v, l, page_table, sb_bias[l]).reshape(n_dec, w_c)
        h_s = _out_proj_s(tail_rows(oa_s), tail_rows(ob_s), tail_rows(oc_s), h_s, out_gain[l], w_out[l], tn)

        i = l // 2
        if is_moe:
            xn_s = _norm(h_s, norm_ffn[l], TAIL)
            h_p, h_s = _moe_layer(xn_p, xn_s, h_p, h_s, moe_router[i], moe_w_gate[i].astype(BF16),
                                  moe_w_up[i].astype(BF16), moe_w_down[i].astype(BF16), n_tok, 512, tf)
        else:
            h_p = _ffn_dense(xn_p, h_p, ffn_w_gate[i], ffn_w_up[i], ffn_w_down[i], tm_big, tf // 2)
            h_s = _ffn_s(h_s, norm_ffn[l], ffn_w_gate[i], ffn_w_up[i], ffn_w_down[i], tf)

        kp_l.append(k_p.reshape(n_batch, seq, h_c, HEAD_DIM))
        vp_l.append(v_p.reshape(n_batch, seq, h_c, HEAD_DIM))
        pp_l.append(proj[:, 2 * w_a:2 * w_a + w_b].reshape(n_batch, seq, w_b)[:, seq - POOL_BUF:])
        ks_l.append(ps[:, o_k:o_v].reshape(n_dec, 1, h_c, HEAD_DIM))
        vs_l.append(ps[:, o_v:o_v + w_c].reshape(n_dec, 1, h_c, HEAD_DIM))
        ps_l.append(jnp.concatenate([state_pool[l][:, 1:], p_s[:, None, :]], axis=1))
        gs_l.append(va_s.reshape(n_dec, 1, w_a // HEAD_DIM, HEAD_DIM))

    y_p = _norm(h_p, norm_final, TAIL)
    y_s = _norm(h_s, norm_final, TAIL)
    return (y_p.reshape(n_batch, seq, d_model), y_s[:n_dec].reshape(n_dec, 1, d_model),
            jnp.stack(kp_l), jnp.stack(vp_l), jnp.stack(pp_l),
            jnp.stack(ks_l), jnp.stack(vs_l), jnp.stack(ps_l), jnp.stack(gs_l))
```

```python
import functools

import jax
import jax.numpy as jnp
from jax import lax
from jax.experimental import pallas as pl
from jax.experimental.pallas import tpu as pltpu

F32 = jnp.float32
BF16 = jnp.bfloat16

HEAD_DIM = 128
CHUNK = 128
POOL_WINDOWS = (2, 4, 8, 16)
POOL_BUF = max(POOL_WINDOWS) - 1
POOL_HALO = 16
TOP_K = 2
EPS = 1e-6
SB_SCALE = HEAD_DIM ** -0.5
LOG2E = 1.4426950408889634
LANES = 128
TAIL = 256
ATTN_HEADS_PER_STEP = 2
CUMSUM_CHUNK = 256
PAGES_PER_STEP = 16
VMEM_LIMIT = 56 * 1024 * 1024


def _cparams(sem):
    return pltpu.CompilerParams(dimension_semantics=sem, vmem_limit_bytes=VMEM_LIMIT)


def _rms(x, g):
    ms = jnp.mean(x * x, axis=-1, keepdims=True)
    return x * lax.rsqrt(ms + EPS) * g


def _softplus(z):
    return jnp.maximum(z, 0.0) + jnp.log1p(jnp.exp(-jnp.abs(z)))


def _split2(x):
    hi = x.astype(BF16)
    return hi, (x - hi.astype(F32)).astype(BF16)


def _split3(x):
    h1 = x.astype(BF16)
    r1 = x - h1.astype(F32)
    h2 = r1.astype(BF16)
    h3 = (r1 - h2.astype(F32)).astype(BF16)
    return h1, h2, h3


def _dot(a, b):
    return jnp.dot(a, b, preferred_element_type=F32)


def _dot_hp(x, w):
    xh, xl = _split2(x)
    wh, wl = _split2(w)
    return _dot(xh, wh) + (_dot(xl, wh) + _dot(xh, wl))


def _dot3(parts, m):
    return sum(_dot(p, m) for p in parts)


def _row_tile(n_rows, cap):
    best = TAIL
    for k in range(1, n_rows // TAIL + 1):
        if n_rows % (k * TAIL) == 0 and k * TAIL <= cap:
            best = k * TAIL
    return best


def _norm_proj_body(x_ref, g_ref, w_ref, om_ref, ok_ref, ov_ref, xn_ref, *, jk, jv):
    j = pl.program_id(1)

    @pl.when(j == 0)
    def _():
        xn_ref[...] = _rms(x_ref[...], g_ref[...]).astype(BF16)

    res = _dot(xn_ref[...], w_ref[...].astype(BF16))

    @pl.when(j < jk)
    def _():
        om_ref[...] = res

    @pl.when(jnp.logical_and(j >= jk, j < jv))
    def _():
        ok_ref[...] = res

    @pl.when(j >= jv)
    def _():
        ov_ref[...] = res


def _norm_proj(h, g, w, tm, tn, o_k, o_v):
    t, d = h.shape
    n = w.shape[1]
    assert o_k % tn == 0 and o_v % tn == 0 and n % tn == 0
    jk, jv, nj = o_k // tn, o_v // tn, n // tn
    body = functools.partial(_norm_proj_body, jk=jk, jv=jv)
    return pl.pallas_call(
        body,
        out_shape=(jax.ShapeDtypeStruct((t, o_k), F32), jax.ShapeDtypeStruct((t, o_v - o_k), F32),
                   jax.ShapeDtypeStruct((t, n - o_v), F32)),
        grid=(t // tm, nj),
        in_specs=[pl.BlockSpec((tm, d), lambda i, j: (i, 0)),
                  pl.BlockSpec((1, d), lambda i, j: (0, 0)),
                  pl.BlockSpec((d, tn), lambda i, j: (0, j))],
        out_specs=(pl.BlockSpec((tm, tn), lambda i, j: (i, jnp.minimum(j, jk - 1))),
                   pl.BlockSpec((tm, tn), lambda i, j: (i, jnp.clip(j - jk, 0, jv - jk - 1))),
                   pl.BlockSpec((tm, tn), lambda i, j: (i, jnp.clip(j - jv, 0, nj - jv - 1)))),
        scratch_shapes=[pltpu.VMEM((tm, d), BF16)],
        compiler_params=_cparams(("parallel", "arbitrary")),
        name="norm_proj",
    )(h, g.reshape(1, d), w)


def _norm_proj_s_body(x_ref, g_ref, w_ref, o_ref):
    o_ref[...] = _dot_hp(_rms(x_ref[...], g_ref[...]), w_ref[...])


def _norm_proj_s(h, g, w, tn):
    t, d = h.shape
    n = w.shape[1]
    return pl.pallas_call(
        _norm_proj_s_body,
        out_shape=jax.ShapeDtypeStruct((t, n), F32),
        grid=(n // tn,),
        in_specs=[pl.BlockSpec((t, d), lambda j: (0, 0)),
                  pl.BlockSpec((1, d), lambda j: (0, 0)),
                  pl.BlockSpec((d, tn), lambda j: (0, j))],
        out_specs=pl.BlockSpec((t, tn), lambda j: (0, j)),
        compiler_params=_cparams(("parallel",)),
        name="norm_proj_sample",
    )(h, g.reshape(1, d), w)


def _mix_ab_body(ua_ref, va_ref, p_ref, pp_ref, vg_ref, ws_ref, bt_ref, pw_ref, ps_ref,
                 oa_ref, ob_ref, buf_ref, *, rows, n_heads):
    i = pl.program_id(1)
    r_i = lax.broadcasted_iota(jnp.int32, (CHUNK, CHUNK), 0)
    c_i = lax.broadcasted_iota(jnp.int32, (CHUNK, CHUNK), 1)
    causal = r_i >= c_i
    for c in range(rows // CHUNK):
        rs = slice(c * CHUNK, (c + 1) * CHUNK)
        for hd in range(n_heads):
            cs = slice(hd * HEAD_DIM, (hd + 1) * HEAD_DIM)
            u = jax.nn.gelu(ua_ref[rs, cs])
            van = _rms(jax.nn.gelu(va_ref[rs, cs]), vg_ref[:, cs])
            w = jnp.where(causal, ws_ref[hd], 0.0).astype(BF16)
            mixed = _dot(w, van.astype(BF16)) + bt_ref[:, hd:hd + 1]
            oa_ref[rs, cs] = u * mixed

    buf_ref[0:POOL_HALO, :] = jnp.where(i == 0, 0.0, pp_ref[...])
    buf_ref[POOL_HALO:POOL_HALO + rows, :] = p_ref[...]
    pos = i * rows + lax.broadcasted_iota(jnp.int32, (rows, 1), 0)
    for g, w in enumerate(POOL_WINDOWS):
        cs = slice(g * LANES, (g + 1) * LANES)
        s = buf_ref[POOL_HALO:POOL_HALO + rows, cs]
        for j in range(1, w):
            s = s + buf_ref[POOL_HALO - j:POOL_HALO - j + rows, cs]
        cnt = jnp.minimum(pos + 1, w).astype(F32)
        d = s / cnt - p_ref[:, cs]
        ob_ref[:, cs] = _dot(d.astype(BF16), pw_ref[g].astype(BF16)) * ps_ref[:, cs]


def _mix_ab_prompt(proj, n_batch, seq, v_gain, w_s, b_t, pool_w, pool_scale, w_a, w_b):
    rows = 256 if seq % 256 == 0 else CHUNK
    nr = seq // rows
    n_heads = w_a // HEAD_DIM
    halo_per_block = rows // POOL_HALO
    ca, cv, cp = 0, 1, 2 * w_a // w_b
    body = functools.partial(_mix_ab_body, rows=rows, n_heads=n_heads)
    return pl.pallas_call(
        body,
        out_shape=(jax.ShapeDtypeStruct((n_batch * seq, w_a), F32),
                   jax.ShapeDtypeStruct((n_batch * seq, w_b), F32)),
        grid=(n_batch, nr),
        in_specs=[pl.BlockSpec((rows, w_a), lambda b, i: (b * nr + i, ca)),
                  pl.BlockSpec((rows, w_a), lambda b, i: (b * nr + i, cv)),
                  pl.BlockSpec((rows, w_b), lambda b, i: (b * nr + i, cp)),
                  pl.BlockSpec((POOL_HALO, w_b),
                               lambda b, i: (jnp.maximum((b * nr + i) * halo_per_block - 1, 0), cp)),
                  pl.BlockSpec((1, w_a), lambda b, i: (0, 0)),
                  pl.BlockSpec(w_s.shape, lambda b, i: (0, 0, 0)),
                  pl.BlockSpec(b_t.shape, lambda b, i: (0, 0)),
                  pl.BlockSpec(pool_w.shape, lambda b, i: (0, 0, 0)),
                  pl.BlockSpec((1, w_b), lambda b, i: (0, 0))],
        out_specs=(pl.BlockSpec((rows, w_a), lambda b, i: (b * nr + i, 0)),
                   pl.BlockSpec((rows, w_b), lambda b, i: (b * nr + i, 0))),
        scratch_shapes=[pltpu.VMEM((POOL_HALO + rows, w_b), F32)],
        compiler_params=_cparams(("parallel", "arbitrary")),
        name="mix_ab_prompt",
    )(proj, proj, proj, proj, v_gain.reshape(1, w_a), w_s, b_t, pool_w, pool_scale.reshape(1, w_b))


def _mix_ab_sample_body(ua_ref, va_ref, p_ref, st_ref, vg_ref, w0_ref, b0_ref, pw_ref, ps_ref,
                        oa_ref, ob_ref, vn_ref, *, n_heads, past_len):
    for hd in range(n_heads):
        cs = slice(hd * HEAD_DIM, (hd + 1) * HEAD_DIM)
        u = jax.nn.gelu(ua_ref[:, cs])
        van = _rms(jax.nn.gelu(va_ref[:, cs]), vg_ref[:, cs])
        vn_ref[:, cs] = van
        oa_ref[:, cs] = u * (w0_ref[:, cs] * van + b0_ref[:, cs])
    for g, w in enumerate(POOL_WINDOWS):
        cs = slice(g * LANES, (g + 1) * LANES)
        s = p_ref[:, cs]
        for j in range(1, w):
            s = s + st_ref[:, POOL_BUF - j, cs]
        cnt = float(min(past_len + 1, w))
        d = s / cnt - p_ref[:, cs]
        ob_ref[:, cs] = _dot_hp(d, pw_ref[g]) * ps_ref[:, cs]


def _mix_ab_sample(ua, va, p, state, v_gain, w0, b0, pool_w, pool_scale, past_len):
    n, w_a = ua.shape
    w_b = p.shape[1]
    body = functools.partial(_mix_ab_sample_body, n_heads=w_a // HEAD_DIM, past_len=past_len)
    return pl.pallas_call(
        body,
        out_shape=(jax.ShapeDtypeStruct((n, w_a), F32), jax.ShapeDtypeStruct((n, w_b), F32),
                   jax.ShapeDtypeStruct((n, w_a), F32)),
        name="mix_ab_sample",
    )(ua, va, p, state, v_gain.reshape(1, w_a), w0, b0, pool_w, pool_scale.reshape(1, w_b))


def _attn_body(bias_ref, q_ref, k_ref, v_ref, o_ref, acc_ref, *car_refs, tq, nh):
    hg = pl.program_id(1)
    qi = pl.program_id(2)
    ch = CUMSUM_CHUNK
    n_ch = tq // ch
    r_i = lax.broadcasted_iota(jnp.int32, (2 * ch, ch), 0)
    c_i = lax.broadcasted_iota(jnp.int32, (2 * ch, ch), 1)
    later2 = (jnp.where(r_i >= ch, r_i - ch, r_i) > c_i).astype(BF16)
    visible = lax.broadcasted_iota(jnp.int32, (tq, tq), 1) < lax.broadcasted_iota(jnp.int32, (tq, tq), 0)
    qs = [q_ref[:, hh * HEAD_DIM:(hh + 1) * HEAD_DIM].astype(BF16) for hh in range(nh)]
    bias2 = [bias_ref[hg * nh + hh] * LOG2E for hh in range(nh)]

    def block(kj, masked):
        off = pl.multiple_of(kj * tq, tq)
        for hh in range(nh):
            cs = slice(hh * HEAD_DIM, (hh + 1) * HEAD_DIM)
            kb = k_ref[pl.ds(off, tq), cs].astype(BF16)
            vb = v_ref[pl.ds(off, tq), cs].astype(BF16)
            z2 = lax.dot_general(qs[hh], kb, (((1,), (1,)), ((), ())),
                                 preferred_element_type=F32) * (SB_SCALE * LOG2E) + bias2[hh]
            sp2 = jnp.maximum(z2, 0.0) + jnp.log2(1.0 + jnp.exp2(-jnp.abs(z2)))
            log2_b = z2 - sp2
            if masked:
                sp2 = jnp.where(visible, sp2, 0.0)
            hi, lo = _split2(sp2)
            stacked = jnp.concatenate(
                [jnp.concatenate([hi[:, c * ch:(c + 1) * ch], lo[:, c * ch:(c + 1) * ch]], axis=1)
                 for c in range(n_ch)], axis=0)
            within = _dot(stacked, later2)
            run = car_refs[hh][...]
            after = [None] * n_ch
            for c in reversed(range(n_ch)):
                after[c] = within[c * tq:(c + 1) * tq, :] + run
                run = run + jnp.sum(sp2[:, c * ch:(c + 1) * ch], axis=1, keepdims=True)
            car_refs[hh][...] = run
            a = jnp.exp2(log2_b - jnp.concatenate(after, axis=1))
            if masked:
                a = jnp.where(visible, a, 0.0)
            acc_ref[:, cs] += _dot(a.astype(BF16), vb)

    acc_ref[...] = jnp.zeros_like(acc_ref)
    for car_ref in car_refs:
        car_ref[...] = jnp.zeros_like(car_ref)
    block(qi, True)

    def body(n, carry):
        block(qi - 1 - n, False)
        return carry

    lax.fori_loop(0, qi, body, 0)
    o_ref[...] = acc_ref[...]


def _attn_prompt(proj, k_all, v_all, sb_bias, n_batch, seq, n_heads, col_q):
    tq = next(t for t in (512, 256, 128) if seq % t == 0)
    nq = seq // tq
    nh = ATTN_HEADS_PER_STEP
    assert n_heads % nh == 0 and col_q % nh == 0
    wq = nh * HEAD_DIM
    body = functools.partial(_attn_body, tq=tq, nh=nh)
    return pl.pallas_call(
        body,
        out_shape=jax.ShapeDtypeStruct((n_batch * seq, n_heads * HEAD_DIM), F32),
        grid_spec=pltpu.PrefetchScalarGridSpec(
            num_scalar_prefetch=1,
            grid=(n_batch, n_heads // nh, nq),
            in_specs=[pl.BlockSpec((tq, wq), lambda b, h, i, s: (b * nq + i, col_q // nh + h)),
                      pl.BlockSpec((seq, wq), lambda b, h, i, s: (b, h)),
                      pl.BlockSpec((seq, wq), lambda b, h, i, s: (b, h))],
            out_specs=pl.BlockSpec((tq, wq), lambda b, h, i, s: (b * nq + i, h)),
            scratch_shapes=[pltpu.VMEM((tq, wq), F32)] + [pltpu.VMEM((tq, 1), F32)] * nh),
        compiler_params=_cparams(("parallel", "parallel", "arbitrary")),
        name="attn_prompt",
    )(sb_bias, proj, k_all, v_all)


def _paged_body(pt_ref, q_ref, bias_ref, later_ref, same_ref, *rest, n_groups, n_heads):
    del pt_ref
    pps = PAGES_PER_STEP
    k_refs = rest[:pps]
    v_refs = rest[pps:2 * pps]
    o_ref, z_ref, a_ref, acc_ref = rest[2 * pps:]
    s = pl.program_id(1)
    page_rows = k_refs[0].shape[1] * n_heads
    lane = lax.broadcasted_iota(jnp.int32, (2 * n_heads, page_rows), 1)
    sub = lax.broadcasted_iota(jnp.int32, (2 * n_heads, page_rows), 0)
    own_head = (lane % n_heads) == (sub % n_heads)
    nt = (((1,), (1,)), ((), ()))

    @pl.when(s < n_groups)
    def _():
        qb = jnp.concatenate(_split2(q_ref[0]), axis=0)
        for i in range(pps):
            kh, kl = _split2(k_refs[i][0].reshape(page_rows, HEAD_DIM))
            s16 = (lax.dot_general(qb, kh, nt, preferred_element_type=F32)
                   + lax.dot_general(qb, kl, nt, preferred_element_type=F32))
            z_ref[pl.ds(s * pps + i, 1), :] = jnp.sum(jnp.where(own_head, s16, 0.0), axis=0, keepdims=True)

    @pl.when(s == n_groups - 1)
    def _():
        n_pages = z_ref.shape[0]
        z = z_ref[...] * SB_SCALE + bias_ref[...]
        sp = _softplus(z)
        log_1mb = -sp
        log_b = z - sp
        parts = _split3(log_1mb)
        within = _dot3(parts, later_ref[...])
        totals = _dot3(parts, same_ref[...])
        p_r = lax.broadcasted_iota(jnp.int32, (n_pages, n_pages), 0)
        p_c = lax.broadcasted_iota(jnp.int32, (n_pages, n_pages), 1)
        later_page = (p_c > p_r).astype(BF16)
        carry = sum(_dot(later_page, t) for t in _split3(totals))
        a_ref[...] = jnp.exp(log_b + within + carry)
        acc_ref[...] = jnp.zeros_like(acc_ref)

    @pl.when(s >= n_groups)
    def _():
        acc = acc_ref[...]
        for i in range(pps):
            ah, al = _split2(a_ref[pl.ds((s - n_groups) * pps + i, 1), :])
            rows = jnp.concatenate([jnp.broadcast_to(ah, (n_heads, page_rows)),
                                    jnp.broadcast_to(al, (n_heads, page_rows))], axis=0)
            am = jnp.where(own_head, rows, jnp.zeros_like(rows))
            vh, vl = _split2(v_refs[i][0].reshape(page_rows, HEAD_DIM))
            acc = acc + (_dot(am, vh) + _dot(am, vl))
        acc_ref[...] = acc

    @pl.when(s == 2 * n_groups - 1)
    def _():
        o_ref[0] = acc_ref[0:n_heads, :] + acc_ref[n_heads:, :]


def _attn_sample(q, cache_k, cache_v, layer, page_table, sb_bias):
    n_dec, n_heads, _ = q.shape
    n_pages = page_table.shape[1]
    page = cache_k.shape[2]
    pps = PAGES_PER_STEP
    assert n_pages % pps == 0
    n_groups = n_pages // pps
    page_rows = page * n_heads
    idx = jnp.arange(page_rows, dtype=jnp.int32)
    same_head = (idx[:, None] % n_heads) == (idx[None, :] % n_heads)
    later = (same_head & (idx[:, None] // n_heads > idx[None, :] // n_heads)).astype(BF16)
    same = same_head.astype(BF16)
    bias_lane = jnp.tile(sb_bias.astype(F32), page).reshape(1, page_rows)

    def k_map(i):
        return lambda b, s, pt: (layer, pt[b, jnp.minimum(s, n_groups - 1) * pps + i], 0, 0, 0)

    def v_map(i):
        return lambda b, s, pt: (layer, pt[b, jnp.maximum(s - n_groups, 0) * pps + i], 0, 0, 0)

    page_block = (None, 1, page, n_heads, HEAD_DIM)
    body = functools.partial(_paged_body, n_groups=n_groups, n_heads=n_heads)
    return pl.pallas_call(
        body,
        out_shape=jax.ShapeDtypeStruct((n_dec, n_heads, HEAD_DIM), F32),
        grid_spec=pltpu.PrefetchScalarGridSpec(
            num_scalar_prefetch=1,
            grid=(n_dec, 2 * n_groups),
            in_specs=([pl.BlockSpec((1, n_heads, HEAD_DIM), lambda b, s, pt: (b, 0, 0)),
                       pl.BlockSpec((1, page_rows), lambda b, s, pt: (0, 0)),
                       pl.BlockSpec((page_rows, page_rows), lambda b, s, pt: (0, 0)),
                       pl.BlockSpec((page_rows, page_rows), lambda b, s, pt: (0, 0))]
                      + [pl.BlockSpec(page_block, k_map(i)) for i in range(pps)]
                      + [pl.BlockSpec(page_block, v_map(i)) for i in range(pps)]),
            out_specs=pl.BlockSpec((1, n_heads, HEAD_DIM), lambda b, s, pt: (b, 0, 0)),
            scratch_shapes=[pltpu.VMEM((n_pages, page_rows), F32),
                            pltpu.VMEM((n_pages, page_rows), F32),
                            pltpu.VMEM((2 * n_heads, HEAD_DIM), F32)]),
        compiler_params=_cparams(("arbitrary", "arbitrary")),
        name="attn_sample",
    )(page_table, q, bias_lane, later, same, *([cache_k] * pps), *([cache_v] * pps))


def _merged(oa_ref, ob_ref, oc_ref, g_ref, w_a, w_b):
    return (_rms(oa_ref[...], g_ref[:, 0:w_a]), _rms(ob_ref[...], g_ref[:, w_a:w_a + w_b]),
            _rms(oc_ref[...], g_ref[:, w_a + w_b:]))


def _out_proj_body(oa_ref, ob_ref, oc_ref, h_ref, g_ref, w_ref, gn_ref, *rest, w_a, w_b, has_tail):
    ho_ref, xn_ref = rest[-2:]

    def prompt_rows():
        a, b, c = _merged(oa_ref, ob_ref, oc_ref, g_ref, w_a, w_b)
        acc = _dot(a.astype(BF16), w_ref[0:w_a, :])
        acc = acc + _dot(b.astype(BF16), w_ref[w_a:w_a + w_b, :])
        acc = acc + _dot(c.astype(BF16), w_ref[w_a + w_b:, :])
        hn = h_ref[...] + acc
        ho_ref[...] = hn
        xn_ref[...] = _rms(hn, gn_ref[...]).astype(xn_ref.dtype)

    if not has_tail:
        prompt_rows()
        return
    is_tail = pl.program_id(0) == pl.num_programs(0) - 1
    pl.when(jnp.logical_not(is_tail))(prompt_rows)

    @pl.when(is_tail)
    def _():
        xn_ref[...] = rest[0][...].astype(xn_ref.dtype)


def _out_proj(o_a, o_b, o_c, h, out_gain, w_out, norm_next, xn_dtype, tm, xn_tail=None):
    t, d = h.shape
    w_a, w_b, w_c = o_a.shape[1], o_b.shape[1], o_c.shape[1]
    d_mix = w_a + w_b + w_c
    has_tail = xn_tail is not None
    n_blocks = t // tm
    body = functools.partial(_out_proj_body, w_a=w_a, w_b=w_b, has_tail=has_tail)
    row = lambda i: (jnp.minimum(i, n_blocks - 1), 0)
    fixed = lambda i: (0, 0)
    tail_specs = [pl.BlockSpec((tm, d), fixed)] if has_tail else []
    tail_args = [xn_tail] if has_tail else []
    return pl.pallas_call(
        body,
        out_shape=(jax.ShapeDtypeStruct((t, d), F32),
                   jax.ShapeDtypeStruct((t + tm if has_tail else t, d), xn_dtype)),
        grid=(n_blocks + 1 if has_tail else n_blocks,),
        in_specs=[pl.BlockSpec((tm, w_a), row), pl.BlockSpec((tm, w_b), row), pl.BlockSpec((tm, w_c), row),
                  pl.BlockSpec((tm, d), row), pl.BlockSpec((1, d_mix), fixed),
                  pl.BlockSpec((d_mix, d), fixed), pl.BlockSpec((1, d), fixed)] + tail_specs,
        out_specs=(pl.BlockSpec((tm, d), row), pl.BlockSpec((tm, d), lambda i: (i, 0))),
        compiler_params=_cparams(("arbitrary",)),
        name="out_proj",
    )(o_a, o_b, o_c, h, out_gain.reshape(1, d_mix), w_out, norm_next.reshape(1, d), *tail_args)


def _out_proj_s_body(oa_ref, ob_ref, oc_ref, h_ref, g_ref, w_ref, ho_ref, *, w_a, w_b):
    merged = jnp.concatenate(_merged(oa_ref, ob_ref, oc_ref, g_ref, w_a, w_b), axis=1)
    ho_ref[...] = h_ref[...] + _dot_hp(merged, w_ref[...])


def _out_proj_s(o_a, o_b, o_c, h, out_gain, w_out, tn):
    t, d = h.shape
    w_a, w_b, w_c = o_a.shape[1], o_b.shape[1], o_c.shape[1]
    d_mix = w_a + w_b + w_c
    body = functools.partial(_out_proj_s_body, w_a=w_a, w_b=w_b)
    fixed = lambda j: (0, 0)
    col = lambda j: (0, j)
    return pl.pallas_call(
        body,
        out_shape=jax.ShapeDtypeStruct((t, d), F32),
        grid=(d // tn,),
        in_specs=[pl.BlockSpec((t, w_a), fixed), pl.BlockSpec((t, w_b), fixed), pl.BlockSpec((t, w_c), fixed),
                  pl.BlockSpec((t, tn), col), pl.BlockSpec((1, d_mix), fixed), pl.BlockSpec((d_mix, tn), col)],
        out_specs=pl.BlockSpec((t, tn), col),
        compiler_params=_cparams(("parallel",)),
        name="out_proj_sample",
    )(o_a, o_b, o_c, h, out_gain.reshape(1, d_mix), w_out)


def _ffn_body(x_ref, h_ref, wg_ref, wu_ref, wd_ref, o_ref):
    @pl.when(pl.program_id(1) == 0)
    def _():
        o_ref[...] = h_ref[...]

    x = x_ref[...]
    hid = (jax.nn.silu(_dot(x, wg_ref[...].astype(BF16))) * _dot(x, wu_ref[...].astype(BF16))).astype(BF16)
    o_ref[...] += _dot(hid, wd_ref[...].astype(BF16))


def _ffn_dense(xn, h, w_gate, w_up, w_down, tm, tf):
    t, d = h.shape
    d_ff = w_gate.shape[1]
    return pl.pallas_call(
        _ffn_body,
        out_shape=jax.ShapeDtypeStruct((t, d), F32),
        grid=(t // tm, d_ff // tf),
        in_specs=[pl.BlockSpec((tm, d), lambda i, f: (i, 0)),
                  pl.BlockSpec((tm, d), lambda i, f: (i, 0)),
                  pl.BlockSpec((d, tf), lambda i, f: (0, f)),
                  pl.BlockSpec((d, tf), lambda i, f: (0, f)),
                  pl.BlockSpec((tf, d), lambda i, f: (f, 0))],
        out_specs=pl.BlockSpec((tm, d), lambda i, f: (i, 0)),
        compiler_params=_cparams(("parallel", "arbitrary")),
        name="ffn_dense",
    )(xn, h, w_gate, w_up, w_down)


def _ffn_s_body(h_ref, gn_ref, wg_ref, wu_ref, wd_ref, o_ref):
    @pl.when(pl.program_id(0) == 0)
    def _():
        o_ref[...] = h_ref[...]

    xn = _rms(h_ref[...], gn_ref[...])
    hid = jax.nn.silu(_dot_hp(xn, wg_ref[...])) * _dot_hp(xn, wu_ref[...])
    o_ref[...] += _dot_hp(hid, wd_ref[...])


def _ffn_s(h, norm, w_gate, w_up, w_down, tf):
    t, d = h.shape
    d_ff = w_gate.shape[1]
    fixed = lambda f: (0, 0)
    return pl.pallas_call(
        _ffn_s_body,
        out_shape=jax.ShapeDtypeStruct((t, d), F32),
        grid=(d_ff // tf,),
        in_specs=[pl.BlockSpec((t, d), fixed), pl.BlockSpec((1, d), fixed),
                  pl.BlockSpec((d, tf), lambda f: (0, f)),
                  pl.BlockSpec((d, tf), lambda f: (0, f)),
                  pl.BlockSpec((tf, d), lambda f: (f, 0))],
        out_specs=pl.BlockSpec((t, d), fixed),
        compiler_params=_cparams(("arbitrary",)),
        name="ffn_sample",
    )(h, norm.reshape(1, d), w_gate, w_up, w_down)


def _router_body(x_ref, rt_ref, meta_ref, cnt_ref, carry_ref, *, tm, n_exp, n_tok):
    i = pl.program_id(0)

    @pl.when(i == 0)
    def _():
        carry_ref[...] = jnp.zeros_like(carry_ref)

    x = x_ref[...]
    lane = lax.broadcasted_iota(jnp.int32, (tm, LANES), 1)
    neg = jnp.float32(-jnp.inf)
    logits = jnp.full((tm, LANES), neg, F32)
    for e in range(n_exp):
        s = jnp.sum(x * rt_ref[e:e + 1, :], axis=-1, keepdims=True)
        logits = jnp.where(lane == e, s, logits)
    m1 = jnp.max(logits, axis=-1, keepdims=True)
    i1 = jnp.min(jnp.where(logits == m1, lane, LANES), axis=-1, keepdims=True)
    rest = jnp.where(lane == i1, neg, logits)
    m2 = jnp.max(rest, axis=-1, keepdims=True)
    i2 = jnp.min(jnp.where(rest == m2, lane, LANES), axis=-1, keepdims=True)
    e2 = jnp.exp(m2 - m1)
    den = 1.0 + e2
    g1 = 1.0 / den
    g2 = e2 / den
    row = i * tm + lax.broadcasted_iota(jnp.int32, (tm, 1), 0)
    chosen = ((lane == i1) | (lane == i2)) & (row < n_tok)
    onehot = jnp.where(chosen, 1.0, 0.0)
    r_i = lax.broadcasted_iota(jnp.int32, (tm, tm), 0)
    c_i = lax.broadcasted_iota(jnp.int32, (tm, tm), 1)
    earlier = (c_i < r_i).astype(BF16)
    before = _dot(earlier, onehot.astype(BF16)) + carry_ref[...]
    r1 = jnp.sum(jnp.where(lane == i1, before, 0.0), axis=-1, keepdims=True)
    r2 = jnp.sum(jnp.where(lane == i2, before, 0.0), axis=-1, keepdims=True)
    carry_ref[...] += jnp.sum(onehot, axis=0, keepdims=True)
    meta = jnp.zeros((tm, LANES), F32)
    for k, val in enumerate((i1.astype(F32), i2.astype(F32), g1, g2, r1, r2)):
        meta = jnp.where(lane == k, val, meta)
    meta_ref[...] = meta
    cnt_ref[...] = carry_ref[...]


def _router(xn, router_t, n_tok):
    tm = TAIL
    t, d = xn.shape
    n_exp = router_t.shape[0]
    body = functools.partial(_router_body, tm=tm, n_exp=n_exp, n_tok=n_tok)
    return pl.pallas_call(
        body,
        out_shape=(jax.ShapeDtypeStruct((t, LANES), F32), jax.ShapeDtypeStruct((1, LANES), F32)),
        grid=(t // tm,),
        in_specs=[pl.BlockSpec((tm, d), lambda i: (i, 0)),
                  pl.BlockSpec((n_exp, d), lambda i: (0, 0))],
        out_specs=(pl.BlockSpec((tm, LANES), lambda i: (i, 0)), pl.BlockSpec((1, LANES), lambda i: (0, 0))),
        scratch_shapes=[pltpu.VMEM((1, LANES), F32)],
        compiler_params=_cparams(("arbitrary",)),
        name="moe_router",
    )(xn, router_t)


def _dispatch_body(tok_ref, used_ref, x_hbm, o_ref, buf, sems, *, tm):
    b = pl.program_id(0)
    n_blocks = pl.num_programs(0)
    slot = b % 2

    def row_copy(blk, sl, r):
        return pltpu.make_async_copy(x_hbm.at[pl.ds(tok_ref[blk * tm + r], 1), :],
                                     buf.at[sl, pl.ds(r, 1), :], sems.at[sl])

    def issue_block(blk, sl):
        def issue(r, carry):
            row_copy(blk, sl, r).start()
            return carry

        lax.fori_loop(0, tm, issue, 0)

    def drain(r, carry):
        row_copy(b, slot, r).wait()
        return carry

    @pl.when(jnp.logical_and(b == 0, used_ref[0] > 0))
    def _():
        issue_block(0, 0)

    nxt = jnp.minimum(b + 1, n_blocks - 1)

    @pl.when(jnp.logical_and(b + 1 < n_blocks, used_ref[nxt] > 0))
    def _():
        issue_block(nxt, 1 - slot)

    @pl.when(used_ref[b] > 0)
    def _():
        lax.fori_loop(0, tm, drain, 0)
        o_ref[...] = buf[slot].astype(BF16)

    @pl.when(used_ref[b] == 0)
    def _():
        o_ref[...] = jnp.zeros_like(o_ref)


def _dispatch(slot_tok, used, xn, tm):
    n_slots = slot_tok.shape[0]
    d = xn.shape[1]
    body = functools.partial(_dispatch_body, tm=tm)
    return pl.pallas_call(
        body,
        out_shape=jax.ShapeDtypeStruct((n_slots, d), BF16),
        grid_spec=pltpu.PrefetchScalarGridSpec(
            num_scalar_prefetch=2,
            grid=(n_slots // tm,),
            in_specs=[pl.BlockSpec(memory_space=pl.ANY)],
            out_specs=pl.BlockSpec((tm, d), lambda b, st, us: (b, 0)),
            scratch_shapes=[pltpu.VMEM((2, tm, d), F32), pltpu.SemaphoreType.DMA((2,))]),
        compiler_params=_cparams(("arbitrary",)),
        name="moe_dispatch",
    )(slot_tok, used, xn)


def _moe_body(be_ref, nb_ref, parts_ref, x_ref, wg_ref, wu_ref, wd_ref, o_ref, *, part):
    del be_ref, nb_ref
    b = pl.program_id(0)
    f = pl.program_id(1)

    @pl.when(f == 0)
    def _():
        o_ref[...] = jnp.zeros_like(o_ref)

    wg = wg_ref[...].astype(BF16)
    wu = wu_ref[...].astype(BF16)
    wd = wd_ref[...].astype(BF16)
    for p in range(x_ref.shape[0] // part):
        rows = slice(p * part, (p + 1) * part)

        @pl.when(p < parts_ref[b])
        def _():
            xb = x_ref[rows, :]
            hid = (jax.nn.silu(_dot(xb, wg)) * _dot(xb, wu)).astype(BF16)
            o_ref[rows, :] += _dot(hid, wd)


def _moe_experts(blk_expert, nb_used, parts, slots, w_gate, w_up, w_down, tm, part, tf):
    n_slots, d = slots.shape
    d_ff = w_gate.shape[2]
    nf = d_ff // tf
    nb = n_slots // tm

    def w_in_map(b, f, be, nu, pr):
        live = b < nu[0]
        return be[jnp.minimum(b, nu[0] - 1)], 0, jnp.where(live, f, nf - 1)

    def w_out_map(b, f, be, nu, pr):
        live = b < nu[0]
        return be[jnp.minimum(b, nu[0] - 1)], jnp.where(live, f, nf - 1), 0

    return pl.pallas_call(
        functools.partial(_moe_body, part=part),
        out_shape=jax.ShapeDtypeStruct((n_slots, d), F32),
        grid_spec=pltpu.PrefetchScalarGridSpec(
            num_scalar_prefetch=3,
            grid=(nb, nf),
            in_specs=[pl.BlockSpec((tm, d), lambda b, f, be, nu, pr: (b, 0)),
                      pl.BlockSpec((None, d, tf), w_in_map),
                      pl.BlockSpec((None, d, tf), w_in_map),
                      pl.BlockSpec((None, tf, d), w_out_map)],
            out_specs=pl.BlockSpec((tm, d), lambda b, f, be, nu, pr: (b, 0))),
        compiler_params=_cparams(("arbitrary", "arbitrary")),
        name="moe_experts",
    )(blk_expert, nb_used, parts, slots, w_gate, w_up, w_down)


def _combine_body(dest_ref, y_hbm, hp_ref, hs_ref, meta_ref, op_ref, os_ref, buf, sems, *, tm, n_tok):
    i = pl.program_id(0)
    n_blocks = pl.num_programs(0)
    is_tail = i == n_blocks - 1
    slot = i % 2

    def row_copy(blk, sl, r, k):
        t = jnp.minimum(blk * tm + r, n_tok - 1)
        return pltpu.make_async_copy(y_hbm.at[pl.ds(dest_ref[t * TOP_K + k], 1), :],
                                     buf.at[sl, k, pl.ds(r, 1), :], sems.at[sl])

    def issue_block(blk, sl):
        def issue(r, carry):
            for k in range(TOP_K):
                row_copy(blk, sl, r, k).start()
            return carry

        lax.fori_loop(0, tm, issue, 0)

    def drain(r, carry):
        for k in range(TOP_K):
            row_copy(i, slot, r, k).wait()
        return carry

    @pl.when(i == 0)
    def _():
        issue_block(0, 0)

    @pl.when(i + 1 < n_blocks)
    def _():
        issue_block(i + 1, 1 - slot)

    lax.fori_loop(0, tm, drain, 0)
    valid = (i * tm + lax.broadcasted_iota(jnp.int32, (tm, 1), 0)) < n_tok
    g1 = jnp.where(valid, meta_ref[:, 2:3], 0.0)
    g2 = jnp.where(valid, meta_ref[:, 3:4], 0.0)
    y = buf[slot, 0] * g1 + buf[slot, 1] * g2

    @pl.when(is_tail)
    def _():
        os_ref[...] = hs_ref[...] + y

    @pl.when(jnp.logical_not(is_tail))
    def _():
        op_ref[...] = hp_ref[...] + y


def _combine(dest, y_slots, h_p, h_s, meta, n_tok):
    tm = TAIL
    n_p, d = h_p.shape
    n_blocks = n_p // tm + 1
    body = functools.partial(_combine_body, tm=tm, n_tok=n_tok)
    return pl.pallas_call(
        body,
        out_shape=(jax.ShapeDtypeStruct((n_p, d), F32), jax.ShapeDtypeStruct((tm, d), F32)),
        grid_spec=pltpu.PrefetchScalarGridSpec(
            num_scalar_prefetch=1,
            grid=(n_blocks,),
            in_specs=[pl.BlockSpec(memory_space=pl.ANY),
                      pl.BlockSpec((tm, d), lambda i, ds: (jnp.minimum(i, n_blocks - 2), 0)),
                      pl.BlockSpec((tm, d), lambda i, ds: (0, 0)),
                      pl.BlockSpec((tm, LANES), lambda i, ds: (i, 0))],
            out_specs=(pl.BlockSpec((tm, d), lambda i, ds: (jnp.minimum(i, n_blocks - 2), 0)),
                       pl.BlockSpec((tm, d), lambda i, ds: (0, 0))),
            scratch_shapes=[pltpu.VMEM((2, TOP_K, tm, d), F32), pltpu.SemaphoreType.DMA((2,))]),
        compiler_params=_cparams(("arbitrary",)),
        name="moe_combine",
    )(dest, y_slots, h_p, h_s, meta)


def _moe_layer(xn, h_p, h_s, router, w_gate, w_up, w_down, n_tok, tm_e, part, tf):
    n_exp = router.shape[1]
    meta, cnt = _router(xn, jnp.transpose(router), n_tok)
    expert = meta[:n_tok, 0:TOP_K].astype(jnp.int32)
    rank = meta[:n_tok, 4:4 + TOP_K].astype(jnp.int32)
    counts = cnt[0, :n_exp].astype(jnp.int32)
    padded = (counts + tm_e - 1) // tm_e * tm_e
    ends = jnp.cumsum(padded)
    starts = ends - padded
    dest = (starts[expert] + rank).reshape(-1)
    n_assign = n_tok * TOP_K
    nb = -(-(n_assign + n_exp * (tm_e - 1)) // tm_e)
    blk_start = jnp.arange(nb, dtype=jnp.int32) * tm_e
    blk_expert = jnp.minimum(jnp.searchsorted(ends, blk_start, side='right'), n_exp - 1).astype(jnp.int32)
    nb_used = (ends[-1:] // tm_e).astype(jnp.int32)
    filled = jnp.clip(starts[blk_expert] + counts[blk_expert] - blk_start, 0, tm_e)
    parts = ((filled + part - 1) // part).astype(jnp.int32)
    per_block = tm_e // part
    part_used = (jnp.arange(nb * per_block, dtype=jnp.int32) % per_block
                 < jnp.repeat(parts, per_block)).astype(jnp.int32)
    tok = jnp.repeat(jnp.arange(n_tok, dtype=jnp.int32), TOP_K)
    slot_tok = jnp.full((nb * tm_e,), n_tok, jnp.int32).at[dest].set(tok)
    slots = _dispatch(slot_tok, part_used, xn, part)
    y_slots = _moe_experts(blk_expert, nb_used, parts, slots, w_gate, w_up, w_down, tm_e, part, tf)
    return _combine(dest, y_slots, h_p, h_s, meta, n_tok)


def _norm_body(x_ref, g_ref, o_ref):
    o_ref[...] = _rms(x_ref[...], g_ref[...])


def _norm(h, g, tm):
    t, d = h.shape
    return pl.pallas_call(
        _norm_body,
        out_shape=jax.ShapeDtypeStruct((t, d), F32),
        grid=(t // tm,),
        in_specs=[pl.BlockSpec((tm, d), lambda i: (i, 0)), pl.BlockSpec((1, d), lambda i: (0, 0))],
        out_specs=pl.BlockSpec((tm, d), lambda i: (i, 0)),
        compiler_params=_cparams(("parallel",)),
        name="row_norm",
    )(h, g.reshape(1, d))


def kernel(x_prompt, x_sample, cache_k, cache_v, state_pool, page_table, norm_mix, w_in, gmlp_v_gain, gmlp_w,
           gmlp_b, pool_w, pool_scale, sb_bias, out_gain, w_out, norm_ffn, ffn_w_gate, ffn_w_up, ffn_w_down,
           moe_router, moe_w_gate, moe_w_up, moe_w_down, norm_final):
    n_batch, seq, d_model = x_prompt.shape
    n_dec, dec_seq, _ = x_sample.shape
    depth = norm_mix.shape[0]
    page = cache_k.shape[2]
    h_c = cache_k.shape[3]
    past_len = page_table.shape[1] * page
    w_a = gmlp_v_gain.shape[1] * HEAD_DIM
    w_b = pool_scale.shape[1]
    w_c = h_c * HEAD_DIM
    d_ff = ffn_w_gate.shape[2]
    assert dec_seq == 1 and n_dec < TAIL and seq % CHUNK == 0 and (n_batch * seq) % TAIL == 0
    assert past_len % CHUNK == 0 and w_a == w_b

    n_p = n_batch * seq
    n_tok = n_p + n_dec
    col_q = (2 * w_a + w_b) // HEAD_DIM
    col_k = col_q + h_c
    col_v = col_k + h_c
    o_q, o_k, o_v = col_q * HEAD_DIM, col_k * HEAD_DIM, col_v * HEAD_DIM
    tm_big = _row_tile(n_p, 1024)
    tf = 512 if d_ff % 512 == 0 else 256
    tn = 512

    def tail_rows(x):
        return jnp.pad(x, ((0, TAIL - n_dec), (0, 0)))

    h_p = x_prompt.reshape(n_p, d_model)
    h_s = tail_rows(x_sample.reshape(n_dec, d_model))

    kp_l, vp_l, pp_l, ks_l, vs_l, ps_l, gs_l = [], [], [], [], [], [], []
    for l in range(depth):
        is_moe = l % 2 == 1
        i = l // 2

        ps = _norm_proj_s(h_s, norm_mix[l], w_in[l], tn)[:n_dec]
        p_s = ps[:, 2 * w_a:2 * w_a + w_b]
        w0 = jnp.repeat(gmlp_w[l, :, 0, 0], HEAD_DIM).reshape(1, w_a)
        b0 = jnp.repeat(gmlp_b[l, :, 0], HEAD_DIM).reshape(1, w_a)
        oa_s, ob_s, va_s = _mix_ab_sample(ps[:, 0:w_a], ps[:, w_a:2 * w_a], p_s, state_pool[l],
                                          gmlp_v_gain[l], w0, b0, pool_w[l], pool_scale[l], past_len)
        q_s = ps[:, o_q:o_k].reshape(n_dec, h_c, HEAD_DIM)
        oc_s = _attn_sample(q_s, cache_k, cache_v, l, page_table, sb_bias[l]).reshape(n_dec, w_c)
        h_s = _out_proj_s(tail_rows(oa_s), tail_rows(ob_s), tail_rows(oc_s), h_s, out_gain[l], w_out[l], tn)

        proj, k_p, v_p = _norm_proj(h_p, norm_mix[l], w_in[l], tm_big, tn, o_k, o_v)
        o_a, o_b = _mix_ab_prompt(proj, n_batch, seq, gmlp_v_gain[l], gmlp_w[l],
                                  jnp.transpose(gmlp_b[l]), pool_w[l], pool_scale[l], w_a, w_b)
        o_c = _attn_prompt(proj, k_p, v_p, sb_bias[l], n_batch, seq, h_c, col_q)

        if is_moe:
            xn_s = _norm(h_s, norm_ffn[l], TAIL)
            h_p, xn = _out_proj(o_a, o_b, o_c, h_p, out_gain[l], w_out[l].astype(BF16), norm_ffn[l],
                                F32, TAIL, xn_tail=xn_s)
            h_p, h_s = _moe_layer(xn, h_p, h_s, moe_router[i], moe_w_gate[i], moe_w_up[i], moe_w_down[i],
                                  n_tok, 1024, 512, tf // 2)
        else:
            h_p, xn_p = _out_proj(o_a, o_b, o_c, h_p, out_gain[l], w_out[l].astype(BF16), norm_ffn[l],
                                  BF16, TAIL)
            h_p = _ffn_dense(xn_p, h_p, ffn_w_gate[i], ffn_w_up[i], ffn_w_down[i], tm_big, tf // 2)
            h_s = _ffn_s(h_s, norm_ffn[l], ffn_w_gate[i], ffn_w_up[i], ffn_w_down[i], tf)

        kp_l.append(k_p.reshape(n_batch, seq, h_c, HEAD_DIM))
        vp_l.append(v_p.reshape(n_batch, seq, h_c, HEAD_DIM))
        pp_l.append(proj[:, 2 * w_a:2 * w_a + w_b].reshape(n_batch, seq, w_b)[:, seq - POOL_BUF:])
        ks_l.append(ps[:, o_k:o_v].reshape(n_dec, 1, h_c, HEAD_DIM))
        vs_l.append(ps[:, o_v:o_v + w_c].reshape(n_dec, 1, h_c, HEAD_DIM))
        ps_l.append(jnp.concatenate([state_pool[l][:, 1:], p_s[:, None, :]], axis=1))
        gs_l.append(va_s.reshape(n_dec, 1, w_a // HEAD_DIM, HEAD_DIM))

    y_p = _norm(h_p, norm_final, TAIL)
    y_s = _norm(h_s, norm_final, TAIL)
    return (y_p.reshape(n_batch, seq, d_model), y_s[:n_dec].reshape(n_dec, 1, d_model),
            jnp.stack(kp_l), jnp.stack(vp_l), jnp.stack(pp_l),
            jnp.stack(ks_l), jnp.stack(vs_l), jnp.stack(ps_l), jnp.stack(gs_l))
```

```python
import functools

import jax
import jax.numpy as jnp
from jax import lax
from jax.experimental import pallas as pl
from jax.experimental.pallas import tpu as pltpu

F32 = jnp.float32
BF16 = jnp.bfloat16

HEAD_DIM = 128
CHUNK = 128
POOL_WINDOWS = (2, 4, 8, 16)
POOL_BUF = max(POOL_WINDOWS) - 1
POOL_HALO = 16
TOP_K = 2
EPS = 1e-6
SB_SCALE = HEAD_DIM ** -0.5
LOG2E = 1.4426950408889634
LANES = 128
TAIL = 256
ATTN_HEADS_PER_STEP = 2
CUMSUM_CHUNK = 256
PAGES_PER_STEP = 16
VMEM_LIMIT = 58 * 1024 * 1024


def _cparams(sem):
    return pltpu.CompilerParams(dimension_semantics=sem, vmem_limit_bytes=VMEM_LIMIT)


def _rms(x, g):
    ms = jnp.mean(x * x, axis=-1, keepdims=True)
    return x * lax.rsqrt(ms + EPS) * g


def _softplus(z):
    return jnp.maximum(z, 0.0) + jnp.log1p(jnp.exp(-jnp.abs(z)))


def _split2(x):
    hi = x.astype(BF16)
    return hi, (x - hi.astype(F32)).astype(BF16)


def _split3(x):
    h1 = x.astype(BF16)
    r1 = x - h1.astype(F32)
    h2 = r1.astype(BF16)
    h3 = (r1 - h2.astype(F32)).astype(BF16)
    return h1, h2, h3


def _dot(a, b):
    return jnp.dot(a, b, preferred_element_type=F32)


def _dot_hp(x, w):
    xh, xl = _split2(x)
    wh, wl = _split2(w)
    return _dot(xh, wh) + (_dot(xl, wh) + _dot(xh, wl))


def _dot3(parts, m):
    return sum(_dot(p, m) for p in parts)


def _row_tile(n_rows, cap):
    best = TAIL
    for k in range(1, n_rows // TAIL + 1):
        if n_rows % (k * TAIL) == 0 and k * TAIL <= cap:
            best = k * TAIL
    return best


def _norm_proj_body(x_ref, g_ref, w_ref, om_ref, ok_ref, ov_ref, xn_ref, *, jk, jv):
    j = pl.program_id(1)

    @pl.when(j == 0)
    def _():
        xn_ref[...] = _rms(x_ref[...], g_ref[...]).astype(BF16)

    res = _dot(xn_ref[...], w_ref[...].astype(BF16))

    @pl.when(j < jk)
    def _():
        om_ref[...] = res

    @pl.when(jnp.logical_and(j >= jk, j < jv))
    def _():
        ok_ref[...] = res

    @pl.when(j >= jv)
    def _():
        ov_ref[...] = res


def _norm_proj(h, g, w, tm, tn, o_k, o_v):
    t, d = h.shape
    n = w.shape[1]
    assert o_k % tn == 0 and o_v % tn == 0 and n % tn == 0
    jk, jv, nj = o_k // tn, o_v // tn, n // tn
    body = functools.partial(_norm_proj_body, jk=jk, jv=jv)
    return pl.pallas_call(
        body,
        out_shape=(jax.ShapeDtypeStruct((t, o_k), F32), jax.ShapeDtypeStruct((t, o_v - o_k), F32),
                   jax.ShapeDtypeStruct((t, n - o_v), F32)),
        grid=(t // tm, nj),
        in_specs=[pl.BlockSpec((tm, d), lambda i, j: (i, 0)),
                  pl.BlockSpec((1, d), lambda i, j: (0, 0)),
                  pl.BlockSpec((d, tn), lambda i, j: (0, j))],
        out_specs=(pl.BlockSpec((tm, tn), lambda i, j: (i, jnp.minimum(j, jk - 1))),
                   pl.BlockSpec((tm, tn), lambda i, j: (i, jnp.clip(j - jk, 0, jv - jk - 1))),
                   pl.BlockSpec((tm, tn), lambda i, j: (i, jnp.clip(j - jv, 0, nj - jv - 1)))),
        scratch_shapes=[pltpu.VMEM((tm, d), BF16)],
        compiler_params=_cparams(("parallel", "arbitrary")),
        name="norm_proj",
    )(h, g.reshape(1, d), w)


def _norm_proj_s_body(x_ref, g_ref, w_ref, o_ref):
    o_ref[...] = _dot_hp(_rms(x_ref[...], g_ref[...]), w_ref[...])


def _norm_proj_s(h, g, w, tn):
    t, d = h.shape
    n = w.shape[1]
    return pl.pallas_call(
        _norm_proj_s_body,
        out_shape=jax.ShapeDtypeStruct((t, n), F32),
        grid=(n // tn,),
        in_specs=[pl.BlockSpec((t, d), lambda j: (0, 0)),
                  pl.BlockSpec((1, d), lambda j: (0, 0)),
                  pl.BlockSpec((d, tn), lambda j: (0, j))],
        out_specs=pl.BlockSpec((t, tn), lambda j: (0, j)),
        compiler_params=_cparams(("parallel",)),
        name="norm_proj_sample",
    )(h, g.reshape(1, d), w)


def _mix_ab_body(ua_ref, va_ref, p_ref, pp_ref, vg_ref, ws_ref, bt_ref, pw_ref, ps_ref,
                 oa_ref, ob_ref, buf_ref, *, rows, n_heads):
    i = pl.program_id(1)
    r_i = lax.broadcasted_iota(jnp.int32, (CHUNK, CHUNK), 0)
    c_i = lax.broadcasted_iota(jnp.int32, (CHUNK, CHUNK), 1)
    causal = r_i >= c_i
    for c in range(rows // CHUNK):
        rs = slice(c * CHUNK, (c + 1) * CHUNK)
        for hd in range(n_heads):
            cs = slice(hd * HEAD_DIM, (hd + 1) * HEAD_DIM)
            u = jax.nn.gelu(ua_ref[rs, cs])
            van = _rms(jax.nn.gelu(va_ref[rs, cs]), vg_ref[:, cs])
            w = jnp.where(causal, ws_ref[hd], 0.0).astype(BF16)
            mixed = _dot(w, van.astype(BF16)) + bt_ref[:, hd:hd + 1]
            oa_ref[rs, cs] = u * mixed

    buf_ref[0:POOL_HALO, :] = jnp.where(i == 0, 0.0, pp_ref[...])
    buf_ref[POOL_HALO:POOL_HALO + rows, :] = p_ref[...]
    pos = i * rows + lax.broadcasted_iota(jnp.int32, (rows, 1), 0)
    for g, w in enumerate(POOL_WINDOWS):
        cs = slice(g * LANES, (g + 1) * LANES)
        s = buf_ref[POOL_HALO:POOL_HALO + rows, cs]
        for j in range(1, w):
            s = s + buf_ref[POOL_HALO - j:POOL_HALO - j + rows, cs]
        cnt = jnp.minimum(pos + 1, w).astype(F32)
        d = s / cnt - p_ref[:, cs]
        ob_ref[:, cs] = _dot(d.astype(BF16), pw_ref[g].astype(BF16)) * ps_ref[:, cs]


def _mix_ab_prompt(proj, n_batch, seq, v_gain, w_s, b_t, pool_w, pool_scale, w_a, w_b):
    rows = 256 if seq % 256 == 0 else CHUNK
    nr = seq // rows
    n_heads = w_a // HEAD_DIM
    halo_per_block = rows // POOL_HALO
    ca, cv, cp = 0, 1, 2 * w_a // w_b
    body = functools.partial(_mix_ab_body, rows=rows, n_heads=n_heads)
    return pl.pallas_call(
        body,
        out_shape=(jax.ShapeDtypeStruct((n_batch * seq, w_a), F32),
                   jax.ShapeDtypeStruct((n_batch * seq, w_b), F32)),
        grid=(n_batch, nr),
        in_specs=[pl.BlockSpec((rows, w_a), lambda b, i: (b * nr + i, ca)),
                  pl.BlockSpec((rows, w_a), lambda b, i: (b * nr + i, cv)),
                  pl.BlockSpec((rows, w_b), lambda b, i: (b * nr + i, cp)),
                  pl.BlockSpec((POOL_HALO, w_b),
                               lambda b, i: (jnp.maximum((b * nr + i) * halo_per_block - 1, 0), cp)),
                  pl.BlockSpec((1, w_a), lambda b, i: (0, 0)),
                  pl.BlockSpec(w_s.shape, lambda b, i: (0, 0, 0)),
                  pl.BlockSpec(b_t.shape, lambda b, i: (0, 0)),
                  pl.BlockSpec(pool_w.shape, lambda b, i: (0, 0, 0)),
                  pl.BlockSpec((1, w_b), lambda b, i: (0, 0))],
        out_specs=(pl.BlockSpec((rows, w_a), lambda b, i: (b * nr + i, 0)),
                   pl.BlockSpec((rows, w_b), lambda b, i: (b * nr + i, 0))),
        scratch_shapes=[pltpu.VMEM((POOL_HALO + rows, w_b), F32)],
        compiler_params=_cparams(("parallel", "arbitrary")),
        name="mix_ab_prompt",
    )(proj, proj, proj, proj, v_gain.reshape(1, w_a), w_s, b_t, pool_w, pool_scale.reshape(1, w_b))


def _mix_ab_sample_body(ua_ref, va_ref, p_ref, st_ref, vg_ref, w0_ref, b0_ref, pw_ref, ps_ref,
                        oa_ref, ob_ref, vn_ref, *, n_heads, past_len):
    for hd in range(n_heads):
        cs = slice(hd * HEAD_DIM, (hd + 1) * HEAD_DIM)
        u = jax.nn.gelu(ua_ref[:, cs])
        van = _rms(jax.nn.gelu(va_ref[:, cs]), vg_ref[:, cs])
        vn_ref[:, cs] = van
        oa_ref[:, cs] = u * (w0_ref[:, cs] * van + b0_ref[:, cs])
    for g, w in enumerate(POOL_WINDOWS):
        cs = slice(g * LANES, (g + 1) * LANES)
        s = p_ref[:, cs]
        for j in range(1, w):
            s = s + st_ref[:, POOL_BUF - j, cs]
        cnt = float(min(past_len + 1, w))
        d = s / cnt - p_ref[:, cs]
        ob_ref[:, cs] = _dot_hp(d, pw_ref[g]) * ps_ref[:, cs]


def _mix_ab_sample(ua, va, p, state, v_gain, w0, b0, pool_w, pool_scale, past_len):
    n, w_a = ua.shape
    w_b = p.shape[1]
    body = functools.partial(_mix_ab_sample_body, n_heads=w_a // HEAD_DIM, past_len=past_len)
    return pl.pallas_call(
        body,
        out_shape=(jax.ShapeDtypeStruct((n, w_a), F32), jax.ShapeDtypeStruct((n, w_b), F32),
                   jax.ShapeDtypeStruct((n, w_a), F32)),
        name="mix_ab_sample",
    )(ua, va, p, state, v_gain.reshape(1, w_a), w0, b0, pool_w, pool_scale.reshape(1, w_b))


def _attn_body(bias_ref, q_ref, k_ref, v_ref, o_ref, acc_ref, *car_refs, tq, nh):
    hg = pl.program_id(1)
    qi = pl.program_id(2)
    ch = CUMSUM_CHUNK
    n_ch = tq // ch
    r_i = lax.broadcasted_iota(jnp.int32, (2 * ch, ch), 0)
    c_i = lax.broadcasted_iota(jnp.int32, (2 * ch, ch), 1)
    later2 = (jnp.where(r_i >= ch, r_i - ch, r_i) > c_i).astype(BF16)
    visible = lax.broadcasted_iota(jnp.int32, (tq, tq), 1) < lax.broadcasted_iota(jnp.int32, (tq, tq), 0)
    qs = [q_ref[:, hh * HEAD_DIM:(hh + 1) * HEAD_DIM].astype(BF16) for hh in range(nh)]
    bias2 = [bias_ref[hg * nh + hh] * LOG2E for hh in range(nh)]

    def block(kj, masked):
        off = pl.multiple_of(kj * tq, tq)
        for hh in range(nh):
            cs = slice(hh * HEAD_DIM, (hh + 1) * HEAD_DIM)
            kb = k_ref[pl.ds(off, tq), cs].astype(BF16)
            vb = v_ref[pl.ds(off, tq), cs].astype(BF16)
            z2 = lax.dot_general(qs[hh], kb, (((1,), (1,)), ((), ())),
                                 preferred_element_type=F32) * (SB_SCALE * LOG2E) + bias2[hh]
            sp2 = jnp.maximum(z2, 0.0) + jnp.log2(1.0 + jnp.exp2(-jnp.abs(z2)))
            log2_b = z2 - sp2
            if masked:
                sp2 = jnp.where(visible, sp2, 0.0)
            hi, lo = _split2(sp2)
            stacked = jnp.concatenate(
                [jnp.concatenate([hi[:, c * ch:(c + 1) * ch], lo[:, c * ch:(c + 1) * ch]], axis=1)
                 for c in range(n_ch)], axis=0)
            within = _dot(stacked, later2)
            run = car_refs[hh][...]
            after = [None] * n_ch
            for c in reversed(range(n_ch)):
                after[c] = within[c * tq:(c + 1) * tq, :] + run
                run = run + jnp.sum(sp2[:, c * ch:(c + 1) * ch], axis=1, keepdims=True)
            car_refs[hh][...] = run
            a = jnp.exp2(log2_b - jnp.concatenate(after, axis=1))
            if masked:
                a = jnp.where(visible, a, 0.0)
            acc_ref[:, cs] += _dot(a.astype(BF16), vb)

    acc_ref[...] = jnp.zeros_like(acc_ref)
    for car_ref in car_refs:
        car_ref[...] = jnp.zeros_like(car_ref)
    block(qi, True)

    def body(n, carry):
        block(qi - 1 - n, False)
        return carry

    lax.fori_loop(0, qi, body, 0)
    o_ref[...] = acc_ref[...]


def _attn_prompt(proj, k_all, v_all, sb_bias, n_batch, seq, n_heads, col_q):
    tq = next(t for t in (512, 256, 128) if seq % t == 0)
    nq = seq // tq
    nh = ATTN_HEADS_PER_STEP
    assert n_heads % nh == 0 and col_q % nh == 0
    wq = nh * HEAD_DIM
    body = functools.partial(_attn_body, tq=tq, nh=nh)
    return pl.pallas_call(
        body,
        out_shape=jax.ShapeDtypeStruct((n_batch * seq, n_heads * HEAD_DIM), F32),
        grid_spec=pltpu.PrefetchScalarGridSpec(
            num_scalar_prefetch=1,
            grid=(n_batch, n_heads // nh, nq),
            in_specs=[pl.BlockSpec((tq, wq), lambda b, h, i, s: (b * nq + i, col_q // nh + h)),
                      pl.BlockSpec((seq, wq), lambda b, h, i, s: (b, h)),
                      pl.BlockSpec((seq, wq), lambda b, h, i, s: (b, h))],
            out_specs=pl.BlockSpec((tq, wq), lambda b, h, i, s: (b * nq + i, h)),
            scratch_shapes=[pltpu.VMEM((tq, wq), F32)] + [pltpu.VMEM((tq, 1), F32)] * nh),
        compiler_params=_cparams(("parallel", "parallel", "arbitrary")),
        name="attn_prompt",
    )(sb_bias, proj, k_all, v_all)


def _paged_body(pt_ref, q_ref, bias_ref, later_ref, same_ref, *rest, n_groups, n_heads):
    del pt_ref
    pps = PAGES_PER_STEP
    k_refs = rest[:pps]
    v_refs = rest[pps:2 * pps]
    o_ref, z_ref, a_ref, acc_ref = rest[2 * pps:]
    s = pl.program_id(1)
    page_rows = k_refs[0].shape[1] * n_heads
    lane = lax.broadcasted_iota(jnp.int32, (2 * n_heads, page_rows), 1)
    sub = lax.broadcasted_iota(jnp.int32, (2 * n_heads, page_rows), 0)
    own_head = (lane % n_heads) == (sub % n_heads)
    nt = (((1,), (1,)), ((), ()))

    @pl.when(s < n_groups)
    def _():
        qb = jnp.concatenate(_split2(q_ref[0]), axis=0)
        for i in range(pps):
            kh, kl = _split2(k_refs[i][0].reshape(page_rows, HEAD_DIM))
            s16 = (lax.dot_general(qb, kh, nt, preferred_element_type=F32)
                   + lax.dot_general(qb, kl, nt, preferred_element_type=F32))
            z_ref[pl.ds(s * pps + i, 1), :] = jnp.sum(jnp.where(own_head, s16, 0.0), axis=0, keepdims=True)

    @pl.when(s == n_groups - 1)
    def _():
        n_pages = z_ref.shape[0]
        z = z_ref[...] * SB_SCALE + bias_ref[...]
        sp = _softplus(z)
        log_1mb = -sp
        log_b = z - sp
        parts = _split3(log_1mb)
        within = _dot3(parts, later_ref[...])
        totals = _dot3(parts, same_ref[...])
        p_r = lax.broadcasted_iota(jnp.int32, (n_pages, n_pages), 0)
        p_c = lax.broadcasted_iota(jnp.int32, (n_pages, n_pages), 1)
        later_page = (p_c > p_r).astype(BF16)
        carry = sum(_dot(later_page, t) for t in _split3(totals))
        a_ref[...] = jnp.exp(log_b + within + carry)
        acc_ref[...] = jnp.zeros_like(acc_ref)

    @pl.when(s >= n_groups)
    def _():
        acc = acc_ref[...]
        for i in range(pps):
            ah, al = _split2(a_ref[pl.ds((s - n_groups) * pps + i, 1), :])
            rows = jnp.concatenate([jnp.broadcast_to(ah, (n_heads, page_rows)),
                                    jnp.broadcast_to(al, (n_heads, page_rows))], axis=0)
            am = jnp.where(own_head, rows, jnp.zeros_like(rows))
            vh, vl = _split2(v_refs[i][0].reshape(page_rows, HEAD_DIM))
            acc = acc + (_dot(am, vh) + _dot(am, vl))
        acc_ref[...] = acc

    @pl.when(s == 2 * n_groups - 1)
    def _():
        o_ref[0] = acc_ref[0:n_heads, :] + acc_ref[n_heads:, :]


def _attn_sample(q, cache_k, cache_v, layer, page_table, sb_bias):
    n_dec, n_heads, _ = q.shape
    n_pages = page_table.shape[1]
    page = cache_k.shape[2]
    pps = PAGES_PER_STEP
    assert n_pages % pps == 0
    n_groups = n_pages // pps
    page_rows = page * n_heads
    idx = jnp.arange(page_rows, dtype=jnp.int32)
    same_head = (idx[:, None] % n_heads) == (idx[None, :] % n_heads)
    later = (same_head & (idx[:, None] // n_heads > idx[None, :] // n_heads)).astype(BF16)
    same = same_head.astype(BF16)
    bias_lane = jnp.tile(sb_bias.astype(F32), page).reshape(1, page_rows)

    def k_map(i):
        return lambda b, s, pt: (layer, pt[b, jnp.minimum(s, n_groups - 1) * pps + i], 0, 0, 0)

    def v_map(i):
        return lambda b, s, pt: (layer, pt[b, jnp.maximum(s - n_groups, 0) * pps + i], 0, 0, 0)

    page_block = (None, 1, page, n_heads, HEAD_DIM)
    body = functools.partial(_paged_body, n_groups=n_groups, n_heads=n_heads)
    return pl.pallas_call(
        body,
        out_shape=jax.ShapeDtypeStruct((n_dec, n_heads, HEAD_DIM), F32),
        grid_spec=pltpu.PrefetchScalarGridSpec(
            num_scalar_prefetch=1,
            grid=(n_dec, 2 * n_groups),
            in_specs=([pl.BlockSpec((1, n_heads, HEAD_DIM), lambda b, s, pt: (b, 0, 0)),
                       pl.BlockSpec((1, page_rows), lambda b, s, pt: (0, 0)),
                       pl.BlockSpec((page_rows, page_rows), lambda b, s, pt: (0, 0)),
                       pl.BlockSpec((page_rows, page_rows), lambda b, s, pt: (0, 0))]
                      + [pl.BlockSpec(page_block, k_map(i)) for i in range(pps)]
                      + [pl.BlockSpec(page_block, v_map(i)) for i in range(pps)]),
            out_specs=pl.BlockSpec((1, n_heads, HEAD_DIM), lambda b, s, pt: (b, 0, 0)),
            scratch_shapes=[pltpu.VMEM((n_pages, page_rows), F32),
                            pltpu.VMEM((n_pages, page_rows), F32),
                            pltpu.VMEM((2 * n_heads, HEAD_DIM), F32)]),
        compiler_params=_cparams(("arbitrary", "arbitrary")),
        name="attn_sample",
    )(page_table, q, bias_lane, later, same, *([cache_k] * pps), *([cache_v] * pps))


def _merged(oa_ref, ob_ref, oc_ref, g_ref, w_a, w_b):
    return (_rms(oa_ref[...], g_ref[:, 0:w_a]), _rms(ob_ref[...], g_ref[:, w_a:w_a + w_b]),
            _rms(oc_ref[...], g_ref[:, w_a + w_b:]))


def _out_proj_body(oa_ref, ob_ref, oc_ref, h_ref, g_ref, w_ref, gn_ref, *rest, w_a, w_b, has_tail):
    ho_ref, xn_ref = rest[-2:]

    def prompt_rows():
        a, b, c = _merged(oa_ref, ob_ref, oc_ref, g_ref, w_a, w_b)
        acc = _dot(a.astype(BF16), w_ref[0:w_a, :])
        acc = acc + _dot(b.astype(BF16), w_ref[w_a:w_a + w_b, :])
        acc = acc + _dot(c.astype(BF16), w_ref[w_a + w_b:, :])
        hn = h_ref[...] + acc
        ho_ref[...] = hn
        xn_ref[...] = _rms(hn, gn_ref[...]).astype(xn_ref.dtype)

    if not has_tail:
        prompt_rows()
        return
    is_tail = pl.program_id(0) == pl.num_programs(0) - 1
    pl.when(jnp.logical_not(is_tail))(prompt_rows)

    @pl.when(is_tail)
    def _():
        xn_ref[...] = rest[0][...].astype(xn_ref.dtype)


def _out_proj(o_a, o_b, o_c, h, out_gain, w_out, norm_next, xn_dtype, tm, xn_tail=None):
    t, d = h.shape
    w_a, w_b, w_c = o_a.shape[1], o_b.shape[1], o_c.shape[1]
    d_mix = w_a + w_b + w_c
    has_tail = xn_tail is not None
    n_blocks = t // tm
    body = functools.partial(_out_proj_body, w_a=w_a, w_b=w_b, has_tail=has_tail)
    row = lambda i: (jnp.minimum(i, n_blocks - 1), 0)
    fixed = lambda i: (0, 0)
    tail_specs = [pl.BlockSpec((tm, d), fixed)] if has_tail else []
    tail_args = [xn_tail] if has_tail else []
    return pl.pallas_call(
        body,
        out_shape=(jax.ShapeDtypeStruct((t, d), F32),
                   jax.ShapeDtypeStruct((t + tm if has_tail else t, d), xn_dtype)),
        grid=(n_blocks + 1 if has_tail else n_blocks,),
        in_specs=[pl.BlockSpec((tm, w_a), row), pl.BlockSpec((tm, w_b), row), pl.BlockSpec((tm, w_c), row),
                  pl.BlockSpec((tm, d), row), pl.BlockSpec((1, d_mix), fixed),
                  pl.BlockSpec((d_mix, d), fixed), pl.BlockSpec((1, d), fixed)] + tail_specs,
        out_specs=(pl.BlockSpec((tm, d), row), pl.BlockSpec((tm, d), lambda i: (i, 0))),
        compiler_params=_cparams(("arbitrary",)),
        name="out_proj",
    )(o_a, o_b, o_c, h, out_gain.reshape(1, d_mix), w_out, norm_next.reshape(1, d), *tail_args)


def _out_proj_s_body(oa_ref, ob_ref, oc_ref, h_ref, g_ref, w_ref, ho_ref, *, w_a, w_b):
    merged = jnp.concatenate(_merged(oa_ref, ob_ref, oc_ref, g_ref, w_a, w_b), axis=1)
    ho_ref[...] = h_ref[...] + _dot_hp(merged, w_ref[...])


def _out_proj_s(o_a, o_b, o_c, h, out_gain, w_out, tn):
    t, d = h.shape
    w_a, w_b, w_c = o_a.shape[1], o_b.shape[1], o_c.shape[1]
    d_mix = w_a + w_b + w_c
    body = functools.partial(_out_proj_s_body, w_a=w_a, w_b=w_b)
    fixed = lambda j: (0, 0)
    col = lambda j: (0, j)
    return pl.pallas_call(
        body,
        out_shape=jax.ShapeDtypeStruct((t, d), F32),
        grid=(d // tn,),
        in_specs=[pl.BlockSpec((t, w_a), fixed), pl.BlockSpec((t, w_b), fixed), pl.BlockSpec((t, w_c), fixed),
                  pl.BlockSpec((t, tn), col), pl.BlockSpec((1, d_mix), fixed), pl.BlockSpec((d_mix, tn), col)],
        out_specs=pl.BlockSpec((t, tn), col),
        compiler_params=_cparams(("parallel",)),
        name="out_proj_sample",
    )(o_a, o_b, o_c, h, out_gain.reshape(1, d_mix), w_out)


def _ffn_body(x_ref, h_ref, wg_ref, wu_ref, wd_ref, o_ref):
    @pl.when(pl.program_id(1) == 0)
    def _():
        o_ref[...] = h_ref[...]

    x = x_ref[...]
    hid = (jax.nn.silu(_dot(x, wg_ref[...].astype(BF16))) * _dot(x, wu_ref[...].astype(BF16))).astype(BF16)
    o_ref[...] += _dot(hid, wd_ref[...].astype(BF16))


def _ffn_dense(xn, h, w_gate, w_up, w_down, tm, tf):
    t, d = h.shape
    d_ff = w_gate.shape[1]
    return pl.pallas_call(
        _ffn_body,
        out_shape=jax.ShapeDtypeStruct((t, d), F32),
        grid=(t // tm, d_ff // tf),
        in_specs=[pl.BlockSpec((tm, d), lambda i, f: (i, 0)),
                  pl.BlockSpec((tm, d), lambda i, f: (i, 0)),
                  pl.BlockSpec((d, tf), lambda i, f: (0, f)),
                  pl.BlockSpec((d, tf), lambda i, f: (0, f)),
                  pl.BlockSpec((tf, d), lambda i, f: (f, 0))],
        out_specs=pl.BlockSpec((tm, d), lambda i, f: (i, 0)),
        compiler_params=_cparams(("parallel", "arbitrary")),
        name="ffn_dense",
    )(xn, h, w_gate, w_up, w_down)


def _ffn_s_body(h_ref, gn_ref, wg_ref, wu_ref, wd_ref, o_ref):
    @pl.when(pl.program_id(0) == 0)
    def _():
        o_ref[...] = h_ref[...]

    xn = _rms(h_ref[...], gn_ref[...])
    hid = jax.nn.silu(_dot_hp(xn, wg_ref[...])) * _dot_hp(xn, wu_ref[...])
    o_ref[...] += _dot_hp(hid, wd_ref[...])


def _ffn_s(h, norm, w_gate, w_up, w_down, tf):
    t, d = h.shape
    d_ff = w_gate.shape[1]
    fixed = lambda f: (0, 0)
    return pl.pallas_call(
        _ffn_s_body,
        out_shape=jax.ShapeDtypeStruct((t, d), F32),
        grid=(d_ff // tf,),
        in_specs=[pl.BlockSpec((t, d), fixed), pl.BlockSpec((1, d), fixed),
                  pl.BlockSpec((d, tf), lambda f: (0, f)),
                  pl.BlockSpec((d, tf), lambda f: (0, f)),
                  pl.BlockSpec((tf, d), lambda f: (f, 0))],
        out_specs=pl.BlockSpec((t, d), fixed),
        compiler_params=_cparams(("arbitrary",)),
        name="ffn_sample",
    )(h, norm.reshape(1, d), w_gate, w_up, w_down)


def _router_body(x_ref, rt_ref, meta_ref, cnt_ref, carry_ref, *, tm, n_exp, n_tok):
    i = pl.program_id(0)

    @pl.when(i == 0)
    def _():
        carry_ref[...] = jnp.zeros_like(carry_ref)

    x = x_ref[...]
    lane = lax.broadcasted_iota(jnp.int32, (tm, LANES), 1)
    neg = jnp.float32(-jnp.inf)
    logits = jnp.full((tm, LANES), neg, F32)
    for e in range(n_exp):
        s = jnp.sum(x * rt_ref[e:e + 1, :], axis=-1, keepdims=True)
        logits = jnp.where(lane == e, s, logits)
    m1 = jnp.max(logits, axis=-1, keepdims=True)
    i1 = jnp.min(jnp.where(logits == m1, lane, LANES), axis=-1, keepdims=True)
    rest = jnp.where(lane == i1, neg, logits)
    m2 = jnp.max(rest, axis=-1, keepdims=True)
    i2 = jnp.min(jnp.where(rest == m2, lane, LANES), axis=-1, keepdims=True)
    e2 = jnp.exp(m2 - m1)
    den = 1.0 + e2
    g1 = 1.0 / den
    g2 = e2 / den
    row = i * tm + lax.broadcasted_iota(jnp.int32, (tm, 1), 0)
    chosen = ((lane == i1) | (lane == i2)) & (row < n_tok)
    onehot = jnp.where(chosen, 1.0, 0.0)
    r_i = lax.broadcasted_iota(jnp.int32, (tm, tm), 0)
    c_i = lax.broadcasted_iota(jnp.int32, (tm, tm), 1)
    earlier = (c_i < r_i).astype(BF16)
    before = _dot(earlier, onehot.astype(BF16)) + carry_ref[...]
    r1 = jnp.sum(jnp.where(lane == i1, before, 0.0), axis=-1, keepdims=True)
    r2 = jnp.sum(jnp.where(lane == i2, before, 0.0), axis=-1, keepdims=True)
    carry_ref[...] += jnp.sum(onehot, axis=0, keepdims=True)
    meta = jnp.zeros((tm, LANES), F32)
    for k, val in enumerate((i1.astype(F32), i2.astype(F32), g1, g2, r1, r2)):
        meta = jnp.where(lane == k, val, meta)
    meta_ref[...] = meta
    cnt_ref[...] = carry_ref[...]


def _router(xn, router_t, n_tok):
    tm = TAIL
    t, d = xn.shape
    n_exp = router_t.shape[0]
    body = functools.partial(_router_body, tm=tm, n_exp=n_exp, n_tok=n_tok)
    return pl.pallas_call(
        body,
        out_shape=(jax.ShapeDtypeStruct((t, LANES), F32), jax.ShapeDtypeStruct((1, LANES), F32)),
        grid=(t // tm,),
        in_specs=[pl.BlockSpec((tm, d), lambda i: (i, 0)),
                  pl.BlockSpec((n_exp, d), lambda i: (0, 0))],
        out_specs=(pl.BlockSpec((tm, LANES), lambda i: (i, 0)), pl.BlockSpec((1, LANES), lambda i: (0, 0))),
        scratch_shapes=[pltpu.VMEM((1, LANES), F32)],
        compiler_params=_cparams(("arbitrary",)),
        name="moe_router",
    )(xn, router_t)


def _dispatch_body(tok_ref, used_ref, x_hbm, o_ref, buf, sems, *, tm):
    b = pl.program_id(0)
    n_blocks = pl.num_programs(0)
    slot = b % 2

    def row_copy(blk, sl, r):
        return pltpu.make_async_copy(x_hbm.at[pl.ds(tok_ref[blk * tm + r], 1), :],
                                     buf.at[sl, pl.ds(r, 1), :], sems.at[sl])

    def issue_block(blk, sl):
        def issue(r, carry):
            row_copy(blk, sl, r).start()
            return carry

        lax.fori_loop(0, tm, issue, 0)

    def drain(r, carry):
        row_copy(b, slot, r).wait()
        return carry

    @pl.when(jnp.logical_and(b == 0, used_ref[0] > 0))
    def _():
        issue_block(0, 0)

    nxt = jnp.minimum(b + 1, n_blocks - 1)

    @pl.when(jnp.logical_and(b + 1 < n_blocks, used_ref[nxt] > 0))
    def _():
        issue_block(nxt, 1 - slot)

    @pl.when(used_ref[b] > 0)
    def _():
        lax.fori_loop(0, tm, drain, 0)
        o_ref[...] = buf[slot].astype(BF16)

    @pl.when(used_ref[b] == 0)
    def _():
        o_ref[...] = jnp.zeros_like(o_ref)


def _dispatch(slot_tok, used, xn, tm):
    n_slots = slot_tok.shape[0]
    d = xn.shape[1]
    body = functools.partial(_dispatch_body, tm=tm)
    return pl.pallas_call(
        body,
        out_shape=jax.ShapeDtypeStruct((n_slots, d), BF16),
        grid_spec=pltpu.PrefetchScalarGridSpec(
            num_scalar_prefetch=2,
            grid=(n_slots // tm,),
            in_specs=[pl.BlockSpec(memory_space=pl.ANY)],
            out_specs=pl.BlockSpec((tm, d), lambda b, st, us: (b, 0)),
            scratch_shapes=[pltpu.VMEM((2, tm, d), F32), pltpu.SemaphoreType.DMA((2,))]),
        compiler_params=_cparams(("arbitrary",)),
        name="moe_dispatch",
    )(slot_tok, used, xn)


def _moe_body(be_ref, nb_ref, parts_ref, x_ref, wg_ref, wu_ref, wd_ref, o_ref, *, part):
    del be_ref, nb_ref
    b = pl.program_id(0)
    f = pl.program_id(1)

    @pl.when(f == 0)
    def _():
        o_ref[...] = jnp.zeros_like(o_ref)

    wg = wg_ref[...].astype(BF16)
    wu = wu_ref[...].astype(BF16)
    wd = wd_ref[...].astype(BF16)
    for p in range(x_ref.shape[0] // part):
        rows = slice(p * part, (p + 1) * part)

        @pl.when(p < parts_ref[b])
        def _():
            xb = x_ref[rows, :]
            hid = (jax.nn.silu(_dot(xb, wg)) * _dot(xb, wu)).astype(BF16)
            o_ref[rows, :] += _dot(hid, wd)


def _moe_experts(blk_expert, nb_used, parts, slots, w_gate, w_up, w_down, tm, part, tf):
    n_slots, d = slots.shape
    d_ff = w_gate.shape[2]
    nf = d_ff // tf
    nb = n_slots // tm

    def w_in_map(b, f, be, nu, pr):
        live = b < nu[0]
        return be[jnp.minimum(b, nu[0] - 1)], 0, jnp.where(live, f, nf - 1)

    def w_out_map(b, f, be, nu, pr):
        live = b < nu[0]
        return be[jnp.minimum(b, nu[0] - 1)], jnp.where(live, f, nf - 1), 0

    return pl.pallas_call(
        functools.partial(_moe_body, part=part),
        out_shape=jax.ShapeDtypeStruct((n_slots, d), F32),
        grid_spec=pltpu.PrefetchScalarGridSpec(
            num_scalar_prefetch=3,
            grid=(nb, nf),
            in_specs=[pl.BlockSpec((tm, d), lambda b, f, be, nu, pr: (b, 0)),
                      pl.BlockSpec((None, d, tf), w_in_map),
                      pl.BlockSpec((None, d, tf), w_in_map),
                      pl.BlockSpec((None, tf, d), w_out_map)],
            out_specs=pl.BlockSpec((tm, d), lambda b, f, be, nu, pr: (b, 0))),
        compiler_params=_cparams(("arbitrary", "arbitrary")),
        name="moe_experts",
    )(blk_expert, nb_used, parts, slots, w_gate, w_up, w_down)


def _combine_body(dest_ref, y_hbm, hp_ref, hs_ref, meta_ref, op_ref, os_ref, buf, sems, *, tm, n_tok):
    i = pl.program_id(0)
    n_blocks = pl.num_programs(0)
    is_tail = i == n_blocks - 1
    slot = i % 2

    def row_copy(blk, sl, r, k):
        t = jnp.minimum(blk * tm + r, n_tok - 1)
        return pltpu.make_async_copy(y_hbm.at[pl.ds(dest_ref[t * TOP_K + k], 1), :],
                                     buf.at[sl, k, pl.ds(r, 1), :], sems.at[sl])

    def issue_block(blk, sl):
        def issue(r, carry):
            for k in range(TOP_K):
                row_copy(blk, sl, r, k).start()
            return carry

        lax.fori_loop(0, tm, issue, 0)

    def drain(r, carry):
        for k in range(TOP_K):
            row_copy(i, slot, r, k).wait()
        return carry

    @pl.when(i == 0)
    def _():
        issue_block(0, 0)

    @pl.when(i + 1 < n_blocks)
    def _():
        issue_block(i + 1, 1 - slot)

    lax.fori_loop(0, tm, drain, 0)
    valid = (i * tm + lax.broadcasted_iota(jnp.int32, (tm, 1), 0)) < n_tok
    g1 = jnp.where(valid, meta_ref[:, 2:3], 0.0)
    g2 = jnp.where(valid, meta_ref[:, 3:4], 0.0)
    y = buf[slot, 0] * g1 + buf[slot, 1] * g2

    @pl.when(is_tail)
    def _():
        os_ref[...] = hs_ref[...] + y

    @pl.when(jnp.logical_not(is_tail))
    def _():
        op_ref[...] = hp_ref[...] + y


def _combine(dest, y_slots, h_p, h_s, meta, n_tok):
    tm = TAIL
    n_p, d = h_p.shape
    n_blocks = n_p // tm + 1
    body = functools.partial(_combine_body, tm=tm, n_tok=n_tok)
    return pl.pallas_call(
        body,
        out_shape=(jax.ShapeDtypeStruct((n_p, d), F32), jax.ShapeDtypeStruct((tm, d), F32)),
        grid_spec=pltpu.PrefetchScalarGridSpec(
            num_scalar_prefetch=1,
            grid=(n_blocks,),
            in_specs=[pl.BlockSpec(memory_space=pl.ANY),
                      pl.BlockSpec((tm, d), lambda i, ds: (jnp.minimum(i, n_blocks - 2), 0)),
                      pl.BlockSpec((tm, d), lambda i, ds: (0, 0)),
                      pl.BlockSpec((tm, LANES), lambda i, ds: (i, 0))],
            out_specs=(pl.BlockSpec((tm, d), lambda i, ds: (jnp.minimum(i, n_blocks - 2), 0)),
                       pl.BlockSpec((tm, d), lambda i, ds: (0, 0))),
            scratch_shapes=[pltpu.VMEM((2, TOP_K, tm, d), F32), pltpu.SemaphoreType.DMA((2,))]),
        compiler_params=_cparams(("arbitrary",)),
        name="moe_combine",
    )(dest, y_slots, h_p, h_s, meta)


def _moe_layer(xn, h_p, h_s, router, w_gate, w_up, w_down, n_tok, tm_e, part, tf):
    n_exp = router.shape[1]
    meta, cnt = _router(xn, jnp.transpose(router), n_tok)
    expert = meta[:n_tok, 0:TOP_K].astype(jnp.int32)
    rank = meta[:n_tok, 4:4 + TOP_K].astype(jnp.int32)
    counts = cnt[0, :n_exp].astype(jnp.int32)
    padded = (counts + tm_e - 1) // tm_e * tm_e
    ends = jnp.cumsum(padded)
    starts = ends - padded
    dest = (starts[expert] + rank).reshape(-1)
    n_assign = n_tok * TOP_K
    nb = -(-(n_assign + n_exp * (tm_e - 1)) // tm_e)
    blk_start = jnp.arange(nb, dtype=jnp.int32) * tm_e
    blk_expert = jnp.minimum(jnp.searchsorted(ends, blk_start, side='right'), n_exp - 1).astype(jnp.int32)
    nb_used = (ends[-1:] // tm_e).astype(jnp.int32)
    filled = jnp.clip(starts[blk_expert] + counts[blk_expert] - blk_start, 0, tm_e)
    parts = ((filled + part - 1) // part).astype(jnp.int32)
    per_block = tm_e // part
    part_used = (jnp.arange(nb * per_block, dtype=jnp.int32) % per_block
                 < jnp.repeat(parts, per_block)).astype(jnp.int32)
    tok = jnp.repeat(jnp.arange(n_tok, dtype=jnp.int32), TOP_K)
    slot_tok = jnp.full((nb * tm_e,), n_tok, jnp.int32).at[dest].set(tok)
    slots = _dispatch(slot_tok, part_used, xn, part)
    y_slots = _moe_experts(blk_expert, nb_used, parts, slots, w_gate, w_up, w_down, tm_e, part, tf)
    return _combine(dest, y_slots, h_p, h_s, meta, n_tok)


def _norm_body(x_ref, g_ref, o_ref):
    o_ref[...] = _rms(x_ref[...], g_ref[...])


def _norm(h, g, tm):
    t, d = h.shape
    return pl.pallas_call(
        _norm_body,
        out_shape=jax.ShapeDtypeStruct((t, d), F32),
        grid=(t // tm,),
        in_specs=[pl.BlockSpec((tm, d), lambda i: (i, 0)), pl.BlockSpec((1, d), lambda i: (0, 0))],
        out_specs=pl.BlockSpec((tm, d), lambda i: (i, 0)),
        compiler_params=_cparams(("parallel",)),
        name="row_norm",
    )(h, g.reshape(1, d))


def kernel(x_prompt, x_sample, cache_k, cache_v, state_pool, page_table, norm_mix, w_in, gmlp_v_gain, gmlp_w,
           gmlp_b, pool_w, pool_scale, sb_bias, out_gain, w_out, norm_ffn, ffn_w_gate, ffn_w_up, ffn_w_down,
           moe_router, moe_w_gate, moe_w_up, moe_w_down, norm_final):
    n_batch, seq, d_model = x_prompt.shape
    n_dec, dec_seq, _ = x_sample.shape
    depth = norm_mix.shape[0]
    page = cache_k.shape[2]
    h_c = cache_k.shape[3]
    past_len = page_table.shape[1] * page
    w_a = gmlp_v_gain.shape[1] * HEAD_DIM
    w_b = pool_scale.shape[1]
    w_c = h_c * HEAD_DIM
    d_ff = ffn_w_gate.shape[2]
    assert dec_seq == 1 and n_dec < TAIL and seq % CHUNK == 0 and (n_batch * seq) % TAIL == 0
    assert past_len % CHUNK == 0 and w_a == w_b

    n_p = n_batch * seq
    n_tok = n_p + n_dec
    col_q = (2 * w_a + w_b) // HEAD_DIM
    col_k = col_q + h_c
    col_v = col_k + h_c
    o_q, o_k, o_v = col_q * HEAD_DIM, col_k * HEAD_DIM, col_v * HEAD_DIM
    tm_big = _row_tile(n_p, 1024)
    tf = 512 if d_ff % 512 == 0 else 256
    tn = 512

    def tail_rows(x):
        return jnp.pad(x, ((0, TAIL - n_dec), (0, 0)))

    h_p = x_prompt.reshape(n_p, d_model)
    h_s = tail_rows(x_sample.reshape(n_dec, d_model))

    kp_l, vp_l, pp_l, ks_l, vs_l, ps_l, gs_l = [], [], [], [], [], [], []
    for l in range(depth):
        is_moe = l % 2 == 1
        i = l // 2

        ps = _norm_proj_s(h_s, norm_mix[l], w_in[l], tn)[:n_dec]
        p_s = ps[:, 2 * w_a:2 * w_a + w_b]
        w0 = jnp.repeat(gmlp_w[l, :, 0, 0], HEAD_DIM).reshape(1, w_a)
        b0 = jnp.repeat(gmlp_b[l, :, 0], HEAD_DIM).reshape(1, w_a)
        oa_s, ob_s, va_s = _mix_ab_sample(ps[:, 0:w_a], ps[:, w_a:2 * w_a], p_s, state_pool[l],
                                          gmlp_v_gain[l], w0, b0, pool_w[l], pool_scale[l], past_len)
        q_s = ps[:, o_q:o_k].reshape(n_dec, h_c, HEAD_DIM)
        oc_s = _attn_sample(q_s, cache_k, cache_v, l, page_table, sb_bias[l]).reshape(n_dec, w_c)
        h_s = _out_proj_s(tail_rows(oa_s), tail_rows(ob_s), tail_rows(oc_s), h_s, out_gain[l], w_out[l], tn)

        proj, k_p, v_p = _norm_proj(h_p, norm_mix[l], w_in[l], tm_big, tn, o_k, o_v)
        o_a, o_b = _mix_ab_prompt(proj, n_batch, seq, gmlp_v_gain[l], gmlp_w[l],
                                  jnp.transpose(gmlp_b[l]), pool_w[l], pool_scale[l], w_a, w_b)
        o_c = _attn_prompt(proj, k_p, v_p, sb_bias[l], n_batch, seq, h_c, col_q)

        if is_moe:
            xn_s = _norm(h_s, norm_ffn[l], TAIL)
            h_p, xn = _out_proj(o_a, o_b, o_c, h_p, out_gain[l], w_out[l].astype(BF16), norm_ffn[l],
                                F32, TAIL, xn_tail=xn_s)
            h_p, h_s = _moe_layer(xn, h_p, h_s, moe_router[i], moe_w_gate[i], moe_w_up[i], moe_w_down[i],
                                  n_tok, 1024, 512, tf)
        else:
            h_p, xn_p = _out_proj(o_a, o_b, o_c, h_p, out_gain[l], w_out[l].astype(BF16), norm_ffn[l],
                                  BF16, TAIL)
            h_p = _ffn_dense(xn_p, h_p, ffn_w_gate[i], ffn_w_up[i], ffn_w_down[i], tm_big, tf // 2)
            h_s = _ffn_s(h_s, norm_ffn[l], ffn_w_gate[i], ffn_w_up[i], ffn_w_down[i], tf)

        kp_l.append(k_p.reshape(n_batch, seq, h_c, HEAD_DIM))
        vp_l.append(v_p.reshape(n_batch, seq, h_c, HEAD_DIM))
        pp_l.append(proj[:, 2 * w_a:2 * w_a + w_b].reshape(n_batch, seq, w_b)[:, seq - POOL_BUF:])
        ks_l.append(ps[:, o_k:o_v].reshape(n_dec, 1, h_c, HEAD_DIM))
        vs_l.append(ps[:, o_v:o_v + w_c].reshape(n_dec, 1, h_c, HEAD_DIM))
        ps_l.append(jnp.concatenate([state_pool[l][:, 1:], p_s[:, None, :]], axis=1))
        gs_l.append(va_s.reshape(n_dec, 1, w_a // HEAD_DIM, HEAD_DIM))

    y_p = _norm(h_p, norm_final, TAIL)
    y_s = _norm(h_s, norm_final, TAIL)
    return (y_p.reshape(n_batch, seq, d_model), y_s[:n_dec].reshape(n_dec, 1, d_model),
            jnp.stack(kp_l), jnp.stack(vp_l), jnp.stack(pp_l),
            jnp.stack(ks_l), jnp.stack(vs_l), jnp.stack(ps_l), jnp.stack(gs_l))
```
